```python
import math
import jax, jax.numpy as jnp
from jax import lax
import numpy as np

D_MODEL = 1024
BATCH = 8
SEQ = 2048
DEPTH = 1
DEC_BATCH = 8
DEC_SEQ = 8192
PAST_LEN = 128

HEAD_DIM = 64
N_HEADS_A = 8
N_KV_A = 2
N_HEADS_B = 8
WIN = 128
BLK = 128
GRID_W = 64
NA_ROWS_MAX = 8
NA_COLS = 16
NA_QCOLS = 16
NA_KCOLS = NA_QCOLS + NA_COLS
D_FF = 2816
ROPE_THETA = 10000.0
EPS = 1e-6
D_A = N_HEADS_A * HEAD_DIM
D_KV_A = N_KV_A * HEAD_DIM
D_B = N_HEADS_B * HEAD_DIM
D_MIX = D_A + D_B
D_IN = D_A + 2 * D_KV_A + 3 * D_B
NEG = -1e30

kernel_name = "hymba_window_gqa_neighbourhood_macaron_encoder"


def rmsnorm(x, g):
    xf = x.astype(jnp.float32)
    y = xf * lax.rsqrt(jnp.mean(xf * xf, axis=-1, keepdims=True) + EPS)
    return (y * g.astype(jnp.float32)).astype(x.dtype)


def swiglu(x, w_gu, w_down):
    g, u = jnp.split(x @ w_gu, 2, axis=-1)
    return (jax.nn.silu(g) * u) @ w_down


def rope(x):
    S = x.shape[1]
    half = HEAD_DIM // 2
    inv = ROPE_THETA ** (-jnp.arange(0, HEAD_DIM, 2, dtype=jnp.float32) / HEAD_DIM)
    ang = jnp.arange(S, dtype=jnp.float32)[:, None] * inv[None, :]
    cos = jnp.cos(ang)[None, :, None, :]
    sin = jnp.sin(ang)[None, :, None, :]
    xf = x.astype(jnp.float32)
    x1, x2 = xf[..., :half], xf[..., half:]
    return jnp.concatenate([x1 * cos - x2 * sin, x2 * cos + x1 * sin], axis=-1).astype(x.dtype)


def window_gqa(q, k, v, sink):
    B, S = q.shape[0], q.shape[1]
    nblk = S // BLK
    G = N_HEADS_A // N_KV_A
    scale = HEAD_DIM ** -0.5
    qb = q.reshape(B, nblk, BLK, N_KV_A, G, HEAD_DIM)
    kp = jnp.pad(k, ((0, 0), (WIN, WIN), (0, 0), (0, 0)))
    vp = jnp.pad(v, ((0, 0), (WIN, WIN), (0, 0), (0, 0)))
    nk = BLK + 2 * WIN
    sink_l = jnp.broadcast_to(sink.astype(jnp.float32).reshape(1, N_KV_A, G, 1, 1), (B, N_KV_A, G, BLK, 1))

    def one_block(bi):
        qi = lax.dynamic_index_in_dim(qb, bi, axis=1, keepdims=False)
        ki = lax.dynamic_slice_in_dim(kp, bi * BLK, nk, axis=1)
        vi = lax.dynamic_slice_in_dim(vp, bi * BLK, nk, axis=1)
        s = jnp.einsum('bqhgd,bkhd->bhgqk', qi, ki).astype(jnp.float32) * scale
        qpos = bi * BLK + jnp.arange(BLK)
        kpos = bi * BLK - WIN + jnp.arange(nk)
        valid = (jnp.abs(kpos[None, :] - qpos[:, None]) <= WIN) & (kpos >= 0)[None, :] & (kpos < S)[None, :]
        s = jnp.where(valid, s, NEG)
        p = jax.nn.softmax(jnp.concatenate([s, sink_l], axis=-1), axis=-1)[..., :nk]
        return jnp.einsum('bhgqk,bkhd->bqhgd', p.astype(vi.dtype), vi)

    out = lax.map(one_block, jnp.arange(nblk))
    return out.transpose(1, 0, 2, 3, 4, 5).reshape(B, S, D_A)


def neighbourhood_attn(q, k, v, rpb):
    B, S, H, hd = q.shape
    rows = S // GRID_W
    kr = min(NA_ROWS_MAX, rows)
    ncb = GRID_W // NA_QCOLS
    scale = HEAD_DIM ** -0.5
    qg = q.reshape(B, rows, ncb, NA_QCOLS, H, hd)
    kg = k.reshape(B, rows, GRID_W, H, hd)
    vg = v.reshape(B, rows, GRID_W, H, hd)
    qcol = np.arange(GRID_W).reshape(ncb, NA_QCOLS)
    c0 = np.clip(qcol - NA_COLS // 2, 0, GRID_W - NA_COLS)
    kc0 = np.clip(np.arange(ncb) * NA_QCOLS - NA_COLS // 2, 0, GRID_W - NA_KCOLS)
    kcol = kc0[:, None] + np.arange(NA_KCOLS)
    col_valid = jnp.asarray((kcol[:, None, :] >= c0[:, :, None]) & (kcol[:, None, :] < c0[:, :, None] + NA_COLS))
    col_idx = jnp.asarray(np.clip(kcol[:, None, :] - qcol[:, :, None], -(NA_COLS - 1), NA_COLS - 1) + NA_COLS - 1)
    kcol_j = jnp.asarray(kcol)

    def one_row(r):
        r0 = jnp.clip(r - kr // 2, 0, rows - kr)
        kb = lax.dynamic_slice_in_dim(kg, r0, kr, axis=1)[:, :, kcol_j]
        vb = lax.dynamic_slice_in_dim(vg, r0, kr, axis=1)[:, :, kcol_j]
        qr = lax.dynamic_index_in_dim(qg, r, axis=1, keepdims=False)
        s = jnp.einsum('bnqhd,bmnkhd->bhnqmk', qr, kb).astype(jnp.float32) * scale
        row_idx = r0 + jnp.arange(kr) - r + NA_ROWS_MAX - 1
        bias = rpb[:, row_idx[None, None, :, None], col_idx[:, :, None, :]]
        s = jnp.where(col_valid[None, None, :, :, None, :], s + bias[None].astype(jnp.float32), NEG)
        p = jax.nn.softmax(s.reshape(B, H, ncb, NA_QCOLS, kr * NA_KCOLS), axis=-1)
        p = p.reshape(B, H, ncb, NA_QCOLS, kr, NA_KCOLS).astype(vb.dtype)
        return jnp.einsum('bhnqmk,bmnkhd->bnqhd', p, vb)

    out = lax.map(one_row, jnp.arange(rows))
    return out.transpose(1, 0, 2, 3, 4, 5).reshape(B, S, D_B)


def encoder_layer(x, ffn1_pre, ffn1_w_gu, ffn1_w_down, ffn1_post, mix_pre, w_in, sink_a, rpb_b,
                  out_norm_a, out_norm_b, w_out, mix_post, ffn2_pre, ffn2_w_gu, ffn2_w_down, ffn2_post, final_norm):
    B, S, _ = x.shape
    x = x + 0.5 * rmsnorm(swiglu(rmsnorm(x, ffn1_pre), ffn1_w_gu, ffn1_w_down), ffn1_post)
    h = rmsnorm(x, mix_pre)
    proj = h @ w_in
    qa, ka, va, qb, kb, vb = jnp.split(proj, np.cumsum([D_A, D_KV_A, D_KV_A, D_B, D_B]), axis=-1)
    qa = rope(qa.reshape(B, S, N_HEADS_A, HEAD_DIM))
    ka = rope(ka.reshape(B, S, N_KV_A, HEAD_DIM))
    va = va.reshape(B, S, N_KV_A, HEAD_DIM)
    oa = window_gqa(qa, ka, va, sink_a)
    ob = neighbourhood_attn(qb.reshape(B, S, N_HEADS_B, HEAD_DIM), kb.reshape(B, S, N_HEADS_B, HEAD_DIM),
                            vb.reshape(B, S, N_HEADS_B, HEAD_DIM), rpb_b)
    mix = jnp.concatenate([rmsnorm(oa, out_norm_a), rmsnorm(ob, out_norm_b)], axis=-1) @ w_out
    x = x + rmsnorm(mix, mix_post)
    x = x + 0.5 * rmsnorm(swiglu(rmsnorm(x, ffn2_pre), ffn2_w_gu, ffn2_w_down), ffn2_post)
    return rmsnorm(x, final_norm)


def setup_inputs(seed: int = 0) -> dict:
    key = jax.random.key(seed)
    ks = jax.random.split(key, 20)
    f32 = jnp.float32

    def nrm(k, shape, scale):
        return jax.random.normal(k, shape, f32) * scale

    def gain(k, n):
        return 1.0 + 0.02 * jax.random.normal(k, (DEPTH, n), f32)

    return {
        "x_prompt": jax.random.normal(ks[0], (BATCH, SEQ, D_MODEL), f32),
        "x_sample": jax.random.normal(ks[1], (DEC_BATCH, DEC_SEQ, D_MODEL), f32),
        "ffn1_pre": gain(ks[2], D_MODEL),
        "ffn1_w_gu": nrm(ks[3], (DEPTH, D_MODEL, 2 * D_FF), D_MODEL ** -0.5),
        "ffn1_w_down": nrm(ks[4], (DEPTH, D_FF, D_MODEL), D_FF ** -0.5),
        "ffn1_post": gain(ks[5], D_MODEL),
        "mix_pre": gain(ks[6], D_MODEL),
        "w_in": nrm(ks[7], (DEPTH, D_MODEL, D_IN), D_MODEL ** -0.5),
        "sink_a": nrm(ks[8], (DEPTH, N_HEADS_A), 0.5),
        "rpb_b": nrm(ks[9], (DEPTH, N_HEADS_B, 2 * NA_ROWS_MAX - 1, 2 * NA_COLS - 1), 0.1),
        "out_norm_a": gain(ks[10], D_A),
        "out_norm_b": gain(ks[11], D_B),
        "w_out": nrm(ks[12], (DEPTH, D_MIX, D_MODEL), D_MIX ** -0.5),
        "mix_post": gain(ks[13], D_MODEL),
        "ffn2_pre": gain(ks[14], D_MODEL),
        "ffn2_w_gu": nrm(ks[15], (DEPTH, D_MODEL, 2 * D_FF), D_MODEL ** -0.5),
        "ffn2_w_down": nrm(ks[16], (DEPTH, D_FF, D_MODEL), D_FF ** -0.5),
        "ffn2_post": gain(ks[17], D_MODEL),
        "final_norm": gain(ks[18], D_MODEL),
    }


def reference(x_prompt, x_sample, ffn1_pre, ffn1_w_gu, ffn1_w_down, ffn1_post, mix_pre, w_in, sink_a, rpb_b,
              out_norm_a, out_norm_b, w_out, mix_post, ffn2_pre, ffn2_w_gu, ffn2_w_down, ffn2_post, final_norm):
    y_prompt = x_prompt
    y_sample = x_sample
    for l in range(DEPTH):
        p = (ffn1_pre[l], ffn1_w_gu[l], ffn1_w_down[l], ffn1_post[l], mix_pre[l], w_in[l], sink_a[l], rpb_b[l],
             out_norm_a[l], out_norm_b[l], w_out[l], mix_post[l], ffn2_pre[l], ffn2_w_gu[l], ffn2_w_down[l],
             ffn2_post[l], final_norm[l])
        y_prompt = encoder_layer(y_prompt, *p)
        y_sample = encoder_layer(y_sample, *p)
    return (y_prompt, y_sample)
```

```python
import functools

import jax
import jax.numpy as jnp
import numpy as np
from jax import lax
from jax.experimental import pallas as pl
from jax.experimental.pallas import tpu as pltpu

D_MODEL = 1024
HEAD_DIM = 64
N_HEADS_A = 8
N_KV_A = 2
N_HEADS_B = 8
WIN = 128
BLK = 128
GRID_W = 64
NA_ROWS = 8
NA_COLS = 16
D_FF = 2816
ROPE_THETA = 10000.0
EPS = 1e-6
D_A = N_HEADS_A * HEAD_DIM
D_KV_A = N_KV_A * HEAD_DIM
D_B = N_HEADS_B * HEAD_DIM
D_IN = D_A + 2 * D_KV_A + 3 * D_B
D_ROPE = D_A + D_KV_A
NEG = -1e30
SCALE = HEAD_DIM ** -0.5

LANES = 128
FFN_CHUNKS = 2
FFN_CHUNK = D_FF // FFN_CHUNKS
TOKEN_TILE = 512
WIN_Q_TILE = 512
NBR_ROWS_PER_STEP = 8
NBR_KEYS = NA_ROWS * GRID_W
VMEM_LIMIT = 56 * 1024 * 1024

BF16 = jnp.bfloat16
F32 = jnp.float32


def _dot(a, b):
    return jnp.dot(a, b, preferred_element_type=F32)


def _dot_nt(a, b):
    return lax.dot_general(a, b, (((1,), (1,)), ((), ())), preferred_element_type=F32)


def _rms(x, g):
    return x * lax.rsqrt(jnp.mean(x * x, axis=-1, keepdims=True) + EPS) * g


def _swiglu(xn, wgu_ref, wd_ref):
    acc = None
    for c in range(FFN_CHUNKS):
        lo = c * FFN_CHUNK
        g = _dot(xn, wgu_ref[:, lo:lo + FFN_CHUNK])
        u = _dot(xn, wgu_ref[:, D_FF + lo:D_FF + lo + FFN_CHUNK])
        h = (g * jax.nn.sigmoid(g) * u).astype(BF16)
        part = _dot(h, wd_ref[lo:lo + FFN_CHUNK, :])
        acc = part if acc is None else acc + part
    return acc


def _ffn_in_kernel(x_ref, cos_ref, sin_ref, pre1_ref, wgu_ref, wd_ref, post1_ref, mixpre_ref, win_ref,
                   x1_ref, qa_ref, ka_ref, va_ref, qb_ref, kb_ref, vb_ref):
    x = x_ref[...]
    f = _swiglu(_rms(x, pre1_ref[...]).astype(BF16), wgu_ref, wd_ref)
    x1 = x + 0.5 * _rms(f, post1_ref[...])
    x1_ref[...] = x1
    proj = _dot(_rms(x1, mixpre_ref[...]).astype(BF16), win_ref[...])

    cos = cos_ref[...]
    sin = sin_ref[...]
    lane = lax.broadcasted_iota(jnp.int32, cos.shape, 1)
    first_half = (lane % HEAD_DIM) < (HEAD_DIM // 2)
    roped = []
    for t in range(D_ROPE // LANES):
        xt = proj[:, t * LANES:(t + 1) * LANES]
        partner = jnp.where(first_half,
                            pltpu.roll(xt, LANES - HEAD_DIM // 2, 1),
                            pltpu.roll(xt, HEAD_DIM // 2, 1))
        roped.append(xt * cos + partner * sin)
    qa_ref[...] = (jnp.concatenate(roped[:D_A // LANES], axis=1) * SCALE).astype(BF16)
    ka_ref[...] = roped[D_A // LANES].astype(BF16)
    o = D_ROPE
    va_ref[...] = proj[:, o:o + D_KV_A].astype(BF16)
    o += D_KV_A
    qb_ref[...] = (proj[:, o:o + D_B] * SCALE).astype(BF16)
    o += D_B
    kb_ref[...] = proj[:, o:o + D_B].astype(BF16)
    o += D_B
    vb_ref[...] = proj[:, o:o + D_B].astype(BF16)


def _const_spec(shape):
    return pl.BlockSpec(shape, lambda *_: (0,) * len(shape), pipeline_mode=pl.Buffered(1))


def _ffn_in(x2d, seq, cos_t, sin_t, pre1, wgu, wd, post1, mixpre, win):
    n_tok = x2d.shape[0]
    tm = TOKEN_TILE
    tiles_per_seq = seq // tm
    tok = lambda w: pl.BlockSpec((tm, w), lambda i: (i, 0))
    rope_spec = pl.BlockSpec((tm, LANES), lambda i: (i % tiles_per_seq, 0))
    vec = _const_spec((1, D_MODEL))
    out_widths = (D_A, D_KV_A, D_KV_A, D_B, D_B, D_B)
    return pl.pallas_call(
        _ffn_in_kernel,
        grid=(n_tok // tm,),
        in_specs=[tok(D_MODEL), rope_spec, rope_spec, vec, _const_spec(wgu.shape), _const_spec(wd.shape), vec, vec,
                  _const_spec(win.shape)],
        out_specs=[tok(D_MODEL)] + [tok(w) for w in out_widths],
        out_shape=[jax.ShapeDtypeStruct((n_tok, D_MODEL), F32)]
        + [jax.ShapeDtypeStruct((n_tok, w), BF16) for w in out_widths],
        compiler_params=pltpu.CompilerParams(dimension_semantics=("parallel",), vmem_limit_bytes=VMEM_LIMIT),
        name="ffn_in",
    )(x2d, cos_t, sin_t, pre1, wgu, wd, post1, mixpre, win)


def _attn_win_kernel(sink_ref, q_ref, k_ref, v_ref, g_ref, o_ref, *, seq):
    i = pl.program_id(1)
    group = N_HEADS_A // N_KV_A
    nk = BLK + 2 * WIN
    rows = group * BLK
    qrow = lax.broadcasted_iota(jnp.int32, (rows, nk), 0) % BLK
    kcol = lax.broadcasted_iota(jnp.int32, (rows, nk), 1)
    rel = kcol - qrow
    head_of_row = lax.broadcasted_iota(jnp.int32, (rows, 1), 0) // BLK
    for j in range(WIN_Q_TILE // BLK):
        q0 = i * WIN_Q_TILE + j * BLK
        start = pl.multiple_of(jnp.clip(q0 - WIN, 0, seq - nk), BLK)
        off = q0 - start
        valid = jnp.abs(rel - off) <= WIN
        kw = k_ref[0, pl.ds(start, nk), :]
        vw = v_ref[0, pl.ds(start, nk), :]
        outs = []
        for hk in range(N_KV_A):
            kh = kw[:, hk * HEAD_DIM:(hk + 1) * HEAD_DIM]
            vh = vw[:, hk * HEAD_DIM:(hk + 1) * HEAD_DIM]
            heads = range(hk * group, (hk + 1) * group)
            qs = jnp.concatenate(
                [q_ref[0, j * BLK:(j + 1) * BLK, h * HEAD_DIM:(h + 1) * HEAD_DIM] for h in heads], axis=0)
            sink = jnp.zeros((rows, 1), F32)
            for gi, h in enumerate(heads):
                sink = jnp.where(head_of_row == gi, sink_ref[h], sink)
            s = jnp.where(valid, _dot_nt(qs, kh), NEG)
            m = jnp.maximum(jnp.max(s, axis=-1, keepdims=True), sink)
            p = jnp.exp(s - m)
            denom = jnp.sum(p, axis=-1, keepdims=True) + jnp.exp(sink - m)
            o = _dot(p.astype(BF16), vh) / denom
            outs += [o[gi * BLK:(gi + 1) * BLK] for gi in range(group)]
        oa = jnp.concatenate(outs, axis=1)
        o_ref[0, j * BLK:(j + 1) * BLK, :] = _rms(oa, g_ref[...]).astype(BF16)


def _attn_win(qa, ka, va, sink, gain):
    b, seq, _ = qa.shape
    kv_spec = pl.BlockSpec((1, seq, D_KV_A), lambda bi, i: (bi, 0, 0), pipeline_mode=pl.Buffered(1))
    q_spec = pl.BlockSpec((1, WIN_Q_TILE, D_A), lambda bi, i: (bi, i, 0))
    return pl.pallas_call(
        functools.partial(_attn_win_kernel, seq=seq),
        grid=(b, seq // WIN_Q_TILE),
        in_specs=[pl.BlockSpec(memory_space=pltpu.SMEM), q_spec, kv_spec, kv_spec, _const_spec((1, D_A))],
        out_specs=q_spec,
        out_shape=jax.ShapeDtypeStruct((b, seq, D_A), BF16),
        compiler_params=pltpu.CompilerParams(dimension_semantics=("parallel", "arbitrary"),
                                             vmem_limit_bytes=VMEM_LIMIT),
        name="attn_win",
    )(sink, qa, ka, va, gain)


def _attn_nbr_kernel(q_ref, k_ref, v_ref, bias_ref, g_ref, o_ref, *, grid_rows):
    i = pl.program_id(1)

    def one_row(rr, carry):
        r = i * NBR_ROWS_PER_STEP + rr
        r0 = jnp.clip(r - NA_ROWS // 2, 0, grid_rows - NA_ROWS)
        shift = r0 - r + NA_ROWS - 1
        kstart = pl.multiple_of(r0 * GRID_W, GRID_W)
        qoff = pl.multiple_of(rr * GRID_W, GRID_W)
        kw = k_ref[0, pl.ds(kstart, NBR_KEYS), :]
        vw = v_ref[0, pl.ds(kstart, NBR_KEYS), :]
        qw = q_ref[0, pl.ds(qoff, GRID_W), :]
        outs = []
        for h in range(N_HEADS_B):
            hs = slice(h * HEAD_DIM, (h + 1) * HEAD_DIM)
            s = _dot_nt(qw[:, hs], kw[:, hs]) + bias_ref[shift, h]
            m = jnp.max(s, axis=-1, keepdims=True)
            p = jnp.exp(s - m)
            denom = jnp.sum(p, axis=-1, keepdims=True)
            outs.append(_dot(p.astype(BF16), vw[:, hs]) / denom)
        ob = jnp.concatenate(outs, axis=1)
        o_ref[0, pl.ds(qoff, GRID_W), :] = _rms(ob, g_ref[...]).astype(BF16)
        return carry

    lax.fori_loop(0, NBR_ROWS_PER_STEP, one_row, 0)


def _nbr_bias_table(rpb):
    qc = np.arange(GRID_W)[:, None]
    kc = np.arange(GRID_W)[None, :]
    c0 = np.clip(qc - NA_COLS // 2, 0, GRID_W - NA_COLS)
    valid = (kc >= c0) & (kc < c0 + NA_COLS)
    col_idx = np.clip(kc - qc, -(NA_COLS - 1), NA_COLS - 1) + NA_COLS - 1
    toeplitz = jnp.where(valid[None, None], rpb.astype(F32)[:, :, col_idx], NEG)
    slabs = [jnp.concatenate([toeplitz[:, t + m] for m in range(NA_ROWS)], axis=-1) for t in range(NA_ROWS)]
    return jnp.stack(slabs, axis=0)


def _attn_nbr(qb, kb, vb, bias, gain):
    b, seq, _ = qb.shape
    grid_rows = seq // GRID_W
    tq = NBR_ROWS_PER_STEP * GRID_W
    kv_spec = pl.BlockSpec((1, seq, D_B), lambda bi, i: (bi, 0, 0), pipeline_mode=pl.Buffered(1))
    q_spec = pl.BlockSpec((1, tq, D_B), lambda bi, i: (bi, i, 0))
    return pl.pallas_call(
        functools.partial(_attn_nbr_kernel, grid_rows=grid_rows),
        grid=(b, seq // tq),
        in_specs=[q_spec, kv_spec, kv_spec, _const_spec(bias.shape), _const_spec((1, D_B))],
        out_specs=q_spec,
        out_shape=jax.ShapeDtypeStruct((b, seq, D_B), BF16),
        compiler_params=pltpu.CompilerParams(dimension_semantics=("parallel", "arbitrary"),
                                             vmem_limit_bytes=VMEM_LIMIT),
        name="attn_nbr",
    )(qb, kb, vb, bias, gain)


def _out_ffn_kernel(x1_ref, oa_ref, ob_ref, wout_ref, mixpost_ref, pre2_ref, wgu_ref, wd_ref, post2_ref, fin_ref,
                    y_ref):
    mix = _dot(jnp.concatenate([oa_ref[...], ob_ref[...]], axis=1), wout_ref[...])
    x2 = x1_ref[...] + _rms(mix, mixpost_ref[...])
    f = _swiglu(_rms(x2, pre2_ref[...]).astype(BF16), wgu_ref, wd_ref)
    x3 = x2 + 0.5 * _rms(f, post2_ref[...])
    y_ref[...] = _rms(x3, fin_ref[...])


def _out_ffn(x1, oa, ob, wout, mixpost, pre2, wgu, wd, post2, fin):
    n_tok = x1.shape[0]
    tm = TOKEN_TILE
    tok = lambda w: pl.BlockSpec((tm, w), lambda i: (i, 0))
    vec = _const_spec((1, D_MODEL))
    return pl.pallas_call(
        _out_ffn_kernel,
        grid=(n_tok // tm,),
        in_specs=[tok(D_MODEL), tok(D_A), tok(D_B), _const_spec(wout.shape), vec, vec, _const_spec(wgu.shape),
                  _const_spec(wd.shape), vec, vec],
        out_specs=tok(D_MODEL),
        out_shape=jax.ShapeDtypeStruct((n_tok, D_MODEL), F32),
        compiler_params=pltpu.CompilerParams(dimension_semantics=("parallel",), vmem_limit_bytes=VMEM_LIMIT),
        name="out_ffn",
    )(x1, oa, ob, wout, mixpost, pre2, wgu, wd, post2, fin)


def _rope_tables(seq):
    half = HEAD_DIM // 2
    inv = ROPE_THETA ** (-jnp.arange(0, HEAD_DIM, 2, dtype=F32) / HEAD_DIM)
    ang = jnp.arange(seq, dtype=F32)[:, None] * inv[None, :]
    cos, sin = jnp.cos(ang), jnp.sin(ang)
    reps = LANES // HEAD_DIM
    return jnp.tile(jnp.concatenate([cos, cos], axis=1), (1, reps)), jnp.tile(
        jnp.concatenate([-sin, sin], axis=1), (1, reps))


def _layer(x, p):
    b, seq, _ = x.shape
    cos_t, sin_t = _rope_tables(seq)
    x1, qa, ka, va, qb, kb, vb = _ffn_in(x.reshape(b * seq, D_MODEL), seq, cos_t, sin_t, p["ffn1_pre"], p["ffn1_w_gu"],
                                         p["ffn1_w_down"], p["ffn1_post"], p["mix_pre"], p["w_in"])
    r3 = lambda a: a.reshape(b, seq, a.shape[-1])
    oa = _attn_win(r3(qa), r3(ka), r3(va), p["sink_a"], p["out_norm_a"])
    ob = _attn_nbr(r3(qb), r3(kb), r3(vb), p["nbr_bias"], p["out_norm_b"])
    y = _out_ffn(x1, oa.reshape(b * seq, D_A), ob.reshape(b * seq, D_B), p["w_out"], p["mix_post"], p["ffn2_pre"],
                 p["ffn2_w_gu"], p["ffn2_w_down"], p["ffn2_post"], p["final_norm"])
    return y.reshape(b, seq, D_MODEL)


def kernel(x_prompt, x_sample, ffn1_pre, ffn1_w_gu, ffn1_w_down, ffn1_post, mix_pre, w_in, sink_a, rpb_b, out_norm_a,
           out_norm_b, w_out, mix_post, ffn2_pre, ffn2_w_gu, ffn2_w_down, ffn2_post, final_norm):
    y_prompt, y_sample = x_prompt, x_sample
    for l in range(ffn1_pre.shape[0]):
        row = lambda a: a[l].astype(F32).reshape(1, -1)
        p = {
            "ffn1_pre": row(ffn1_pre), "ffn1_post": row(ffn1_post), "mix_pre": row(mix_pre),
            "out_norm_a": row(out_norm_a), "out_norm_b": row(out_norm_b), "mix_post": row(mix_post),
            "ffn2_pre": row(ffn2_pre), "ffn2_post": row(ffn2_post), "final_norm": row(final_norm),
            "ffn1_w_gu": ffn1_w_gu[l].astype(BF16), "ffn1_w_down": ffn1_w_down[l].astype(BF16),
            "ffn2_w_gu": ffn2_w_gu[l].astype(BF16), "ffn2_w_down": ffn2_w_down[l].astype(BF16),
            "w_in": w_in[l].astype(BF16), "w_out": w_out[l].astype(BF16),
            "sink_a": sink_a[l].astype(F32), "nbr_bias": _nbr_bias_table(rpb_b[l]),
        }
        y_prompt = _layer(y_prompt, p)
        y_sample = _layer(y_sample, p)
    return (y_prompt, y_sample)
```

```python
import functools

import jax
import jax.numpy as jnp
import numpy as np
from jax import lax
from jax.experimental import pallas as pl
from jax.experimental.pallas import tpu as pltpu

D_MODEL = 1024
HEAD_DIM = 64
N_HEADS_A = 8
N_KV_A = 2
N_HEADS_B = 8
WIN = 128
BLK = 128
GRID_W = 64
NA_ROWS = 8
NA_COLS = 16
D_FF = 2816
ROPE_THETA = 10000.0
EPS = 1e-6
D_A = N_HEADS_A * HEAD_DIM
D_KV_A = N_KV_A * HEAD_DIM
D_B = N_HEADS_B * HEAD_DIM
D_IN = D_A + 2 * D_KV_A + 3 * D_B
D_ROPE = D_A + D_KV_A
NEG = -1e30
SCALE = HEAD_DIM ** -0.5

LANES = 128
FFN_CHUNKS = 2
FFN_CHUNK = D_FF // FFN_CHUNKS
TOKEN_TILE = 512
WIN_Q_TILE = 512
NBR_ROWS_PER_STEP = 8
NBR_ROW_GROUP = 2
NBR_KEYS = NA_ROWS * GRID_W
VMEM_LIMIT = 56 * 1024 * 1024

BF16 = jnp.bfloat16
F32 = jnp.float32


def _dot(a, b):
    return jnp.dot(a, b, preferred_element_type=F32)


def _dot_nt(a, b):
    return lax.dot_general(a, b, (((1,), (1,)), ((), ())), preferred_element_type=F32)


def _rms(x, g):
    return x * lax.rsqrt(jnp.mean(x * x, axis=-1, keepdims=True) + EPS) * g


def _swiglu(xn, wgu_ref, wd_ref):
    acc = None
    for c in range(FFN_CHUNKS):
        lo = c * FFN_CHUNK
        g = _dot(xn, wgu_ref[:, lo:lo + FFN_CHUNK])
        u = _dot(xn, wgu_ref[:, D_FF + lo:D_FF + lo + FFN_CHUNK])
        h = (g * jax.nn.sigmoid(g) * u).astype(BF16)
        part = _dot(h, wd_ref[lo:lo + FFN_CHUNK, :])
        acc = part if acc is None else acc + part
    return acc


def _ffn_in_kernel(x_ref, cos_ref, sin_ref, pre1_ref, wgu_ref, wd_ref, post1_ref, mixpre_ref, win_ref,
                   x1_ref, qa_ref, ka_ref, va_ref, qb_ref, kb_ref, vb_ref):
    x = x_ref[...]
    f = _swiglu(_rms(x, pre1_ref[...]).astype(BF16), wgu_ref, wd_ref)
    x1 = x + 0.5 * _rms(f, post1_ref[...])
    x1_ref[...] = x1
    proj = _dot(_rms(x1, mixpre_ref[...]).astype(BF16), win_ref[...])

    cos = cos_ref[...]
    sin = sin_ref[...]
    lane = lax.broadcasted_iota(jnp.int32, cos.shape, 1)
    first_half = (lane % HEAD_DIM) < (HEAD_DIM // 2)
    roped = []
    for t in range(D_ROPE // LANES):
        xt = proj[:, t * LANES:(t + 1) * LANES]
        partner = jnp.where(first_half,
                            pltpu.roll(xt, LANES - HEAD_DIM // 2, 1),
                            pltpu.roll(xt, HEAD_DIM // 2, 1))
        roped.append(xt * cos + partner * sin)
    qa_ref[...] = (jnp.concatenate(roped[:D_A // LANES], axis=1) * SCALE).astype(BF16)
    ka_ref[...] = roped[D_A // LANES].astype(BF16)
    o = D_ROPE
    va_ref[...] = proj[:, o:o + D_KV_A].astype(BF16)
    o += D_KV_A
    qb_ref[...] = (proj[:, o:o + D_B] * SCALE).astype(BF16)
    o += D_B
    kb_ref[...] = proj[:, o:o + D_B].astype(BF16)
    o += D_B
    vb_ref[...] = proj[:, o:o + D_B].astype(BF16)


def _const_spec(shape):
    return pl.BlockSpec(shape, lambda *_: (0,) * len(shape), pipeline_mode=pl.Buffered(1))


def _ffn_in(x2d, seq, cos_t, sin_t, pre1, wgu, wd, post1, mixpre, win):
    n_tok = x2d.shape[0]
    tm = TOKEN_TILE
    tiles_per_seq = seq // tm
    tok = lambda w: pl.BlockSpec((tm, w), lambda i: (i, 0))
    rope_spec = pl.BlockSpec((tm, LANES), lambda i: (i % tiles_per_seq, 0))
    vec = _const_spec((1, D_MODEL))
    out_widths = (D_A, D_KV_A, D_KV_A, D_B, D_B, D_B)
    return pl.pallas_call(
        _ffn_in_kernel,
        grid=(n_tok // tm,),
        in_specs=[tok(D_MODEL), rope_spec, rope_spec, vec, _const_spec(wgu.shape), _const_spec(wd.shape), vec, vec,
                  _const_spec(win.shape)],
        out_specs=[tok(D_MODEL)] + [tok(w) for w in out_widths],
        out_shape=[jax.ShapeDtypeStruct((n_tok, D_MODEL), F32)]
        + [jax.ShapeDtypeStruct((n_tok, w), BF16) for w in out_widths],
        compiler_params=pltpu.CompilerParams(dimension_semantics=("parallel",), vmem_limit_bytes=VMEM_LIMIT),
        name="ffn_in",
    )(x2d, cos_t, sin_t, pre1, wgu, wd, post1, mixpre, win)


def _attn_win_kernel(sink_ref, q_ref, k_ref, v_ref, g_ref, o_ref, *, seq):
    i = pl.program_id(1)
    nk = BLK + 2 * WIN
    n_tiles = D_A // LANES
    n_blocks = WIN_Q_TILE // BLK
    lower = lax.broadcasted_iota(jnp.int32, (BLK, LANES), 1) < HEAD_DIM
    rel = lax.broadcasted_iota(jnp.int32, (BLK, nk), 1) - lax.broadcasted_iota(jnp.int32, (BLK, nk), 0)
    block_of_row = lax.broadcasted_iota(jnp.int32, (2 * n_tiles * BLK, 1), 0) // BLK
    sink = jnp.zeros((2 * n_tiles * BLK, 1), F32)
    for t in range(n_tiles):
        for u in range(2):
            sink = jnp.where(block_of_row == 2 * t + u, sink_ref[t + u * n_tiles], sink)

    scores, v_windows = [], []
    for j in range(n_blocks):
        q0 = i * WIN_Q_TILE + j * BLK
        start = pl.multiple_of(jnp.clip(q0 - WIN, 0, seq - nk), BLK)
        mask = jnp.where(jnp.abs(rel - (q0 - start)) <= WIN, 0.0, NEG)
        stacked = []
        for t in range(n_tiles):
            qt = q_ref[0, j * BLK:(j + 1) * BLK, t * LANES:(t + 1) * LANES]
            stacked += [jnp.where(lower, qt, jnp.zeros_like(qt)), jnp.where(lower, jnp.zeros_like(qt), qt)]
        s = _dot_nt(jnp.concatenate(stacked, axis=0), k_ref[0, pl.ds(start, nk), :])
        scores.append(s + jnp.concatenate([mask] * (2 * n_tiles), axis=0))
        v_windows.append(v_ref[0, pl.ds(start, nk), :])

    for j in range(n_blocks):
        s = scores[j]
        m = jnp.maximum(jnp.max(s, axis=-1, keepdims=True), sink)
        p = jnp.exp(s - m)
        denom = jnp.sum(p, axis=-1, keepdims=True) + jnp.exp(sink - m)
        o = _dot(p.astype(BF16), v_windows[j]) * (1.0 / denom)
        tiles = [jnp.where(lower, o[2 * t * BLK:(2 * t + 1) * BLK], o[(2 * t + 1) * BLK:(2 * t + 2) * BLK])
                 for t in range(n_tiles)]
        oa = jnp.concatenate(tiles, axis=1)
        o_ref[0, j * BLK:(j + 1) * BLK, :] = _rms(oa, g_ref[...]).astype(BF16)


def _attn_win(qa, ka, va, sink, gain):
    b, seq, _ = qa.shape
    kv_spec = pl.BlockSpec((1, seq, D_KV_A), lambda bi, i: (bi, 0, 0), pipeline_mode=pl.Buffered(1))
    q_spec = pl.BlockSpec((1, WIN_Q_TILE, D_A), lambda bi, i: (bi, i, 0))
    return pl.pallas_call(
        functools.partial(_attn_win_kernel, seq=seq),
        grid=(b, seq // WIN_Q_TILE),
        in_specs=[pl.BlockSpec(memory_space=pltpu.SMEM), q_spec, kv_spec, kv_spec, _const_spec((1, D_A))],
        out_specs=q_spec,
        out_shape=jax.ShapeDtypeStruct((b, seq, D_A), BF16),
        compiler_params=pltpu.CompilerParams(dimension_semantics=("parallel", "arbitrary"),
                                             vmem_limit_bytes=VMEM_LIMIT),
        name="attn_win",
    )(sink, qa, ka, va, gain)


def _attn_nbr_kernel(q_ref, k_ref, v_ref, bias_ref, g_ref, o_ref, *, grid_rows):
    i = pl.program_id(1)
    n_tiles = D_B // LANES
    lower = lax.broadcasted_iota(jnp.int32, (GRID_W, LANES), 1) < HEAD_DIM

    def row_group(gi, carry):
        scores, v_windows, q_offsets = [], [], []
        for rr in range(NBR_ROW_GROUP):
            row = gi * NBR_ROW_GROUP + rr
            r = i * NBR_ROWS_PER_STEP + row
            r0 = jnp.clip(r - NA_ROWS // 2, 0, grid_rows - NA_ROWS)
            shift = r0 - r + NA_ROWS - 1
            kstart = pl.multiple_of(r0 * GRID_W, GRID_W)
            qoff = pl.multiple_of(row * GRID_W, GRID_W)
            q_offsets.append(qoff)
            for t in range(n_tiles):
                ts = slice(t * LANES, (t + 1) * LANES)
                qt = q_ref[0, pl.ds(qoff, GRID_W), ts]
                qs = jnp.concatenate([jnp.where(lower, qt, jnp.zeros_like(qt)),
                                      jnp.where(lower, jnp.zeros_like(qt), qt)], axis=0)
                scores.append(_dot_nt(qs, k_ref[0, pl.ds(kstart, NBR_KEYS), ts]) + bias_ref[shift, t])
                v_windows.append(v_ref[0, pl.ds(kstart, NBR_KEYS), ts])
        for rr in range(NBR_ROW_GROUP):
            tiles = []
            for t in range(n_tiles):
                s = scores[rr * n_tiles + t]
                m = jnp.max(s, axis=-1, keepdims=True)
                p = jnp.exp(s - m)
                denom = jnp.sum(p, axis=-1, keepdims=True)
                o = _dot(p.astype(BF16), v_windows[rr * n_tiles + t]) * (1.0 / denom)
                tiles.append(jnp.where(lower, o[:GRID_W], o[GRID_W:]))
            ob = jnp.concatenate(tiles, axis=1)
            o_ref[0, pl.ds(q_offsets[rr], GRID_W), :] = _rms(ob, g_ref[...]).astype(BF16)
        return carry

    lax.fori_loop(0, NBR_ROWS_PER_STEP // NBR_ROW_GROUP, row_group, 0)


def _nbr_bias_table(rpb):
    qc = np.arange(GRID_W)[:, None]
    kc = np.arange(GRID_W)[None, :]
    c0 = np.clip(qc - NA_COLS // 2, 0, GRID_W - NA_COLS)
    valid = (kc >= c0) & (kc < c0 + NA_COLS)
    col_idx = np.clip(kc - qc, -(NA_COLS - 1), NA_COLS - 1) + NA_COLS - 1
    toeplitz = jnp.where(valid[None, None], rpb.astype(F32)[:, :, col_idx], NEG)
    slabs = [jnp.concatenate([toeplitz[:, t + m] for m in range(NA_ROWS)], axis=-1) for t in range(NA_ROWS)]
    return jnp.stack(slabs, axis=0).reshape(NA_ROWS, N_HEADS_B // 2, 2 * GRID_W, NBR_KEYS)


def _attn_nbr(qb, kb, vb, bias, gain):
    b, seq, _ = qb.shape
    grid_rows = seq // GRID_W
    tq = NBR_ROWS_PER_STEP * GRID_W
    kv_spec = pl.BlockSpec((1, seq, D_B), lambda bi, i: (bi, 0, 0), pipeline_mode=pl.Buffered(1))
    q_spec = pl.BlockSpec((1, tq, D_B), lambda bi, i: (bi, i, 0))
    return pl.pallas_call(
        functools.partial(_attn_nbr_kernel, grid_rows=grid_rows),
        grid=(b, seq // tq),
        in_specs=[q_spec, kv_spec, kv_spec, _const_spec(bias.shape), _const_spec((1, D_B))],
        out_specs=q_spec,
        out_shape=jax.ShapeDtypeStruct((b, seq, D_B), BF16),
        compiler_params=pltpu.CompilerParams(dimension_semantics=("parallel", "arbitrary"),
                                             vmem_limit_bytes=VMEM_LIMIT),
        name="attn_nbr",
    )(qb, kb, vb, bias, gain)


def _out_ffn_kernel(x1_ref, oa_ref, ob_ref, wout_ref, mixpost_ref, pre2_ref, wgu_ref, wd_ref, post2_ref, fin_ref,
                    y_ref):
    mix = _dot(jnp.concatenate([oa_ref[...], ob_ref[...]], axis=1), wout_ref[...])
    x2 = x1_ref[...] + _rms(mix, mixpost_ref[...])
    f = _swiglu(_rms(x2, pre2_ref[...]).astype(BF16), wgu_ref, wd_ref)
    x3 = x2 + 0.5 * _rms(f, post2_ref[...])
    y_ref[...] = _rms(x3, fin_ref[...])


def _out_ffn(x1, oa, ob, wout, mixpost, pre2, wgu, wd, post2, fin):
    n_tok = x1.shape[0]
    tm = TOKEN_TILE
    tok = lambda w: pl.BlockSpec((tm, w), lambda i: (i, 0))
    vec = _const_spec((1, D_MODEL))
    return pl.pallas_call(
        _out_ffn_kernel,
        grid=(n_tok // tm,),
        in_specs=[tok(D_MODEL), tok(D_A), tok(D_B), _const_spec(wout.shape), vec, vec, _const_spec(wgu.shape),
                  _const_spec(wd.shape), vec, vec],
        out_specs=tok(D_MODEL),
        out_shape=jax.ShapeDtypeStruct((n_tok, D_MODEL), F32),
        compiler_params=pltpu.CompilerParams(dimension_semantics=("parallel",), vmem_limit_bytes=VMEM_LIMIT),
        name="out_ffn",
    )(x1, oa, ob, wout, mixpost, pre2, wgu, wd, post2, fin)


def _rope_tables(seq):
    half = HEAD_DIM // 2
    inv = ROPE_THETA ** (-jnp.arange(0, HEAD_DIM, 2, dtype=F32) / HEAD_DIM)
    ang = jnp.arange(seq, dtype=F32)[:, None] * inv[None, :]
    cos, sin = jnp.cos(ang), jnp.sin(ang)
    reps = LANES // HEAD_DIM
    return jnp.tile(jnp.concatenate([cos, cos], axis=1), (1, reps)), jnp.tile(
        jnp.concatenate([-sin, sin], axis=1), (1, reps))


def _layer(x, p):
    b, seq, _ = x.shape
    cos_t, sin_t = _rope_tables(seq)
    x1, qa, ka, va, qb, kb, vb = _ffn_in(x.reshape(b * seq, D_MODEL), seq, cos_t, sin_t, p["ffn1_pre"], p["ffn1_w_gu"],
                                         p["ffn1_w_down"], p["ffn1_post"], p["mix_pre"], p["w_in"])
    r3 = lambda a: a.reshape(b, seq, a.shape[-1])
    oa = _attn_win(r3(qa), r3(ka), r3(va), p["sink_a"], p["out_norm_a"])
    ob = _attn_nbr(r3(qb), r3(kb), r3(vb), p["nbr_bias"], p["out_norm_b"])
    y = _out_ffn(x1, oa.reshape(b * seq, D_A), ob.reshape(b * seq, D_B), p["w_out"], p["mix_post"], p["ffn2_pre"],
                 p["ffn2_w_gu"], p["ffn2_w_down"], p["ffn2_post"], p["final_norm"])
    return y.reshape(b, seq, D_MODEL)


def kernel(x_prompt, x_sample, ffn1_pre, ffn1_w_gu, ffn1_w_down, ffn1_post, mix_pre, w_in, sink_a, rpb_b, out_norm_a,
           out_norm_b, w_out, mix_post, ffn2_pre, ffn2_w_gu, ffn2_w_down, ffn2_post, final_norm):
    y_prompt, y_sample = x_prompt, x_sample
    half = N_HEADS_A // 2
    head_order = [h for t in range(half) for h in (t, t + half)]
    perm = np.concatenate([np.arange(h * HEAD_DIM, (h + 1) * HEAD_DIM) for h in head_order])
    in_cols = np.concatenate([perm, np.arange(D_A, D_IN)])
    out_rows = np.concatenate([perm, np.arange(D_A, D_A + D_B)])
    for l in range(ffn1_pre.shape[0]):
        row = lambda a: a[l].astype(F32).reshape(1, -1)
        p = {
            "ffn1_pre": row(ffn1_pre), "ffn1_post": row(ffn1_post), "mix_pre": row(mix_pre),
            "out_norm_a": row(out_norm_a)[:, perm], "out_norm_b": row(out_norm_b), "mix_post": row(mix_post),
            "ffn2_pre": row(ffn2_pre), "ffn2_post": row(ffn2_post), "final_norm": row(final_norm),
            "ffn1_w_gu": ffn1_w_gu[l].astype(BF16), "ffn1_w_down": ffn1_w_down[l].astype(BF16),
            "ffn2_w_gu": ffn2_w_gu[l].astype(BF16), "ffn2_w_down": ffn2_w_down[l].astype(BF16),
            "w_in": w_in[l][:, in_cols].astype(BF16), "w_out": w_out[l][out_rows, :].astype(BF16),
            "sink_a": sink_a[l].astype(F32), "nbr_bias": _nbr_bias_table(rpb_b[l]),
        }
        y_prompt = _layer(y_prompt, p)
        y_sample = _layer(y_sample, p)
    return (y_prompt, y_sample)
```

```python
import functools

import jax
import jax.numpy as jnp
import numpy as np
from jax import lax
from jax.experimental import pallas as pl
from jax.experimental.pallas import tpu as pltpu

D_MODEL = 1024
HEAD_DIM = 64
N_HEADS_A = 8
N_KV_A = 2
N_HEADS_B = 8
WIN = 128
BLK = 128
GRID_W = 64
NA_ROWS = 8
NA_COLS = 16
D_FF = 2816
ROPE_THETA = 10000.0
EPS = 1e-6
D_A = N_HEADS_A * HEAD_DIM
D_KV_A = N_KV_A * HEAD_DIM
D_B = N_HEADS_B * HEAD_DIM
D_IN = D_A + 2 * D_KV_A + 3 * D_B
D_ROPE = D_A + D_KV_A
NEG = -1e30
SCALE = HEAD_DIM ** -0.5

LANES = 128
MXU_DIM = 256
FFN_CHUNKS = ((0, 6 * MXU_DIM), (6 * MXU_DIM, 5 * MXU_DIM))
TOKEN_TILE = 512
TOKEN_SUBTILES = 2
WIN_Q_TILE = 512
NBR_ROWS_PER_STEP = 8
NBR_ROW_GROUP = 2
NBR_KEYS = NA_ROWS * GRID_W
VMEM_LIMIT = 56 * 1024 * 1024

BF16 = jnp.bfloat16
F32 = jnp.float32

assert sum(w for _, w in FFN_CHUNKS) == D_FF


def _dot(a, b):
    return jnp.dot(a, b, preferred_element_type=F32)


def _dot_nt(a, b):
    return lax.dot_general(a, b, (((1,), (1,)), ((), ())), preferred_element_type=F32)


def _rms(x, g):
    return x * lax.rsqrt(jnp.mean(x * x, axis=-1, keepdims=True) + EPS) * g


def _subtiles(ref):
    rows = ref.shape[0] // TOKEN_SUBTILES
    return [ref[k * rows:(k + 1) * rows, :] for k in range(TOKEN_SUBTILES)]


def _store_subtiles(ref, values):
    rows = ref.shape[0] // TOKEN_SUBTILES
    for k, v in enumerate(values):
        ref[k * rows:(k + 1) * rows, :] = v.astype(ref.dtype)


def _swiglu(xns, wgu_ref, wd_ref):
    accs = [None] * len(xns)
    for lo, width in FFN_CHUNKS:
        hs = []
        for xn in xns:
            g = _dot(xn, wgu_ref[:, lo:lo + width])
            u = _dot(xn, wgu_ref[:, D_FF + lo:D_FF + lo + width])
            hs.append((g * jax.nn.sigmoid(g) * u).astype(BF16))
        for k, h in enumerate(hs):
            part = _dot(h, wd_ref[lo:lo + width, :])
            accs[k] = part if accs[k] is None else accs[k] + part
    return accs


def _rope_qkv(proj, cos, sin, first_half):
    roped = []
    for t in range(D_ROPE // LANES):
        xt = proj[:, t * LANES:(t + 1) * LANES]
        partner = jnp.where(first_half,
                            pltpu.roll(xt, LANES - HEAD_DIM // 2, 1),
                            pltpu.roll(xt, HEAD_DIM // 2, 1))
        roped.append(xt * cos + partner * sin)
    o = D_ROPE
    va = proj[:, o:o + D_KV_A]
    o += D_KV_A
    qb = proj[:, o:o + D_B] * SCALE
    o += D_B
    kb = proj[:, o:o + D_B]
    o += D_B
    vb = proj[:, o:o + D_B]
    return jnp.concatenate(roped[:D_A // LANES], axis=1) * SCALE, roped[D_A // LANES], va, qb, kb, vb


def _ffn_in_kernel(x_ref, cos_ref, sin_ref, pre1_ref, wgu_ref, wd_ref, post1_ref, mixpre_ref, win_ref,
                   x1_ref, qa_ref, ka_ref, va_ref, qb_ref, kb_ref, vb_ref):
    xs = _subtiles(x_ref)
    fs = _swiglu([_rms(x, pre1_ref[...]).astype(BF16) for x in xs], wgu_ref, wd_ref)
    x1s = [x + 0.5 * _rms(f, post1_ref[...]) for x, f in zip(xs, fs)]
    _store_subtiles(x1_ref, x1s)
    projs = [_dot(_rms(x1, mixpre_ref[...]).astype(BF16), win_ref[...]) for x1 in x1s]

    lane = lax.broadcasted_iota(jnp.int32, (xs[0].shape[0], LANES), 1)
    first_half = (lane % HEAD_DIM) < (HEAD_DIM // 2)
    outs = [_rope_qkv(proj, cos, sin, first_half)
            for proj, cos, sin in zip(projs, _subtiles(cos_ref), _subtiles(sin_ref))]
    for ref, values in zip((qa_ref, ka_ref, va_ref, qb_ref, kb_ref, vb_ref), zip(*outs)):
        _store_subtiles(ref, values)


def _const_spec(shape):
    return pl.BlockSpec(shape, lambda *_: (0,) * len(shape), pipeline_mode=pl.Buffered(1))


def _ffn_in(x2d, seq, cos_t, sin_t, pre1, wgu, wd, post1, mixpre, win):
    n_tok = x2d.shape[0]
    tm = TOKEN_TILE
    tiles_per_seq = seq // tm
    tok = lambda w: pl.BlockSpec((tm, w), lambda i: (i, 0))
    rope_spec = pl.BlockSpec((tm, LANES), lambda i: (i % tiles_per_seq, 0))
    vec = _const_spec((1, D_MODEL))
    out_widths = (D_A, D_KV_A, D_KV_A, D_B, D_B, D_B)
    return pl.pallas_call(
        _ffn_in_kernel,
        grid=(n_tok // tm,),
        in_specs=[tok(D_MODEL), rope_spec, rope_spec, vec, _const_spec(wgu.shape), _const_spec(wd.shape), vec, vec,
                  _const_spec(win.shape)],
        out_specs=[tok(D_MODEL)] + [tok(w) for w in out_widths],
        out_shape=[jax.ShapeDtypeStruct((n_tok, D_MODEL), F32)]
        + [jax.ShapeDtypeStruct((n_tok, w), BF16) for w in out_widths],
        compiler_params=pltpu.CompilerParams(dimension_semantics=("parallel",), vmem_limit_bytes=VMEM_LIMIT),
        name="ffn_in",
    )(x2d, cos_t, sin_t, pre1, wgu, wd, post1, mixpre, win)


def _attn_win_kernel(sink_ref, q_ref, k_ref, v_ref, g_ref, o_ref, *, seq):
    i = pl.program_id(1)
    nk = BLK + 2 * WIN
    n_tiles = D_A // LANES
    n_blocks = WIN_Q_TILE // BLK
    lower = lax.broadcasted_iota(jnp.int32, (BLK, LANES), 1) < HEAD_DIM
    rel = lax.broadcasted_iota(jnp.int32, (BLK, nk), 1) - lax.broadcasted_iota(jnp.int32, (BLK, nk), 0)
    block_of_row = lax.broadcasted_iota(jnp.int32, (2 * n_tiles * BLK, 1), 0) // BLK
    sink = jnp.zeros((2 * n_tiles * BLK, 1), F32)
    for t in range(n_tiles):
        for u in range(2):
            sink = jnp.where(block_of_row == 2 * t + u, sink_ref[t + u * n_tiles], sink)

    scores, v_windows = [], []
    for j in range(n_blocks):
        q0 = i * WIN_Q_TILE + j * BLK
        start = pl.multiple_of(jnp.clip(q0 - WIN, 0, seq - nk), BLK)
        mask = jnp.where(jnp.abs(rel - (q0 - start)) <= WIN, 0.0, NEG)
        stacked = []
        for t in range(n_tiles):
            qt = q_ref[0, j * BLK:(j + 1) * BLK, t * LANES:(t + 1) * LANES]
            stacked += [jnp.where(lower, qt, jnp.zeros_like(qt)), jnp.where(lower, jnp.zeros_like(qt), qt)]
        s = _dot_nt(jnp.concatenate(stacked, axis=0), k_ref[0, pl.ds(start, nk), :])
        scores.append(s + jnp.concatenate([mask] * (2 * n_tiles), axis=0))
        v_windows.append(v_ref[0, pl.ds(start, nk), :])

    for j in range(n_blocks):
        s = scores[j]
        m = jnp.maximum(jnp.max(s, axis=-1, keepdims=True), sink)
        p = jnp.exp(s - m)
        denom = jnp.sum(p, axis=-1, keepdims=True) + jnp.exp(sink - m)
        o = _dot(p.astype(BF16), v_windows[j]) * (1.0 / denom)
        tiles = [jnp.where(lower, o[2 * t * BLK:(2 * t + 1) * BLK], o[(2 * t + 1) * BLK:(2 * t + 2) * BLK])
                 for t in range(n_tiles)]
        oa = jnp.concatenate(tiles, axis=1)
        o_ref[0, j * BLK:(j + 1) * BLK, :] = _rms(oa, g_ref[...]).astype(BF16)


def _attn_win(qa, ka, va, sink, gain):
    b, seq, _ = qa.shape
    kv_spec = pl.BlockSpec((1, seq, D_KV_A), lambda bi, i: (bi, 0, 0), pipeline_mode=pl.Buffered(1))
    q_spec = pl.BlockSpec((1, WIN_Q_TILE, D_A), lambda bi, i: (bi, i, 0))
    return pl.pallas_call(
        functools.partial(_attn_win_kernel, seq=seq),
        grid=(b, seq // WIN_Q_TILE),
        in_specs=[pl.BlockSpec(memory_space=pltpu.SMEM), q_spec, kv_spec, kv_spec, _const_spec((1, D_A))],
        out_specs=q_spec,
        out_shape=jax.ShapeDtypeStruct((b, seq, D_A), BF16),
        compiler_params=pltpu.CompilerParams(dimension_semantics=("parallel", "arbitrary"),
                                             vmem_limit_bytes=VMEM_LIMIT),
        name="attn_win",
    )(sink, qa, ka, va, gain)


def _attn_nbr_kernel(q_ref, k_ref, v_ref, bias_ref, g_ref, o_ref, *, grid_rows):
    i = pl.program_id(1)
    n_tiles = D_B // LANES
    lower = lax.broadcasted_iota(jnp.int32, (GRID_W, LANES), 1) < HEAD_DIM

    def row_group(gi, carry):
        scores, v_windows, q_offsets = [], [], []
        for rr in range(NBR_ROW_GROUP):
            row = gi * NBR_ROW_GROUP + rr
            r = i * NBR_ROWS_PER_STEP + row
            r0 = jnp.clip(r - NA_ROWS // 2, 0, grid_rows - NA_ROWS)
            shift = r0 - r + NA_ROWS - 1
            kstart = pl.multiple_of(r0 * GRID_W, GRID_W)
            qoff = pl.multiple_of(row * GRID_W, GRID_W)
            q_offsets.append(qoff)
            for t in range(n_tiles):
                ts = slice(t * LANES, (t + 1) * LANES)
                qt = q_ref[0, pl.ds(qoff, GRID_W), ts]
                qs = jnp.concatenate([jnp.where(lower, qt, jnp.zeros_like(qt)),
                                      jnp.where(lower, jnp.zeros_like(qt), qt)], axis=0)
                scores.append(_dot_nt(qs, k_ref[0, pl.ds(kstart, NBR_KEYS), ts]) + bias_ref[shift, t])
                v_windows.append(v_ref[0, pl.ds(kstart, NBR_KEYS), ts])
        for rr in range(NBR_ROW_GROUP):
            tiles = []
            for t in range(n_tiles):
                s = scores[rr * n_tiles + t]
                m = jnp.max(s, axis=-1, keepdims=True)
                p = jnp.exp(s - m)
                denom = jnp.sum(p, axis=-1, keepdims=True)
                o = _dot(p.astype(BF16), v_windows[rr * n_tiles + t]) * (1.0 / denom)
                tiles.append(jnp.where(lower, o[:GRID_W], o[GRID_W:]))
            ob = jnp.concatenate(tiles, axis=1)
            o_ref[0, pl.ds(q_offsets[rr], GRID_W), :] = _rms(ob, g_ref[...]).astype(BF16)
        return carry

    lax.fori_loop(0, NBR_ROWS_PER_STEP // NBR_ROW_GROUP, row_group, 0)


def _nbr_bias_table(rpb):
    qc = np.arange(GRID_W)[:, None]
    kc = np.arange(GRID_W)[None, :]
    c0 = np.clip(qc - NA_COLS // 2, 0, GRID_W - NA_COLS)
    valid = (kc >= c0) & (kc < c0 + NA_COLS)
    col_idx = np.clip(kc - qc, -(NA_COLS - 1), NA_COLS - 1) + NA_COLS - 1
    toeplitz = jnp.where(valid[None, None], rpb.astype(F32)[:, :, col_idx], NEG)
    slabs = [jnp.concatenate([toeplitz[:, t + m] for m in range(NA_ROWS)], axis=-1) for t in range(NA_ROWS)]
    return jnp.stack(slabs, axis=0).reshape(NA_ROWS, N_HEADS_B // 2, 2 * GRID_W, NBR_KEYS)


def _attn_nbr(qb, kb, vb, bias, gain):
    b, seq, _ = qb.shape
    grid_rows = seq // GRID_W
    tq = NBR_ROWS_PER_STEP * GRID_W
    kv_spec = pl.BlockSpec((1, seq, D_B), lambda bi, i: (bi, 0, 0), pipeline_mode=pl.Buffered(1))
    q_spec = pl.BlockSpec((1, tq, D_B), lambda bi, i: (bi, i, 0))
    return pl.pallas_call(
        functools.partial(_attn_nbr_kernel, grid_rows=grid_rows),
        grid=(b, seq // tq),
        in_specs=[q_spec, kv_spec, kv_spec, _const_spec(bias.shape), _const_spec((1, D_B))],
        out_specs=q_spec,
        out_shape=jax.ShapeDtypeStruct((b, seq, D_B), BF16),
        compiler_params=pltpu.CompilerParams(dimension_semantics=("parallel", "arbitrary"),
                                             vmem_limit_bytes=VMEM_LIMIT),
        name="attn_nbr",
    )(qb, kb, vb, bias, gain)


def _out_ffn_kernel(x1_ref, oa_ref, ob_ref, wout_ref, mixpost_ref, pre2_ref, wgu_ref, wd_ref, post2_ref, fin_ref,
                    y_ref):
    mixes = [_dot(jnp.concatenate([oa, ob], axis=1), wout_ref[...])
             for oa, ob in zip(_subtiles(oa_ref), _subtiles(ob_ref))]
    x2s = [x1 + _rms(mix, mixpost_ref[...]) for x1, mix in zip(_subtiles(x1_ref), mixes)]
    fs = _swiglu([_rms(x2, pre2_ref[...]).astype(BF16) for x2 in x2s], wgu_ref, wd_ref)
    x3s = [x2 + 0.5 * _rms(f, post2_ref[...]) for x2, f in zip(x2s, fs)]
    _store_subtiles(y_ref, [_rms(x3, fin_ref[...]) for x3 in x3s])


def _out_ffn(x1, oa, ob, wout, mixpost, pre2, wgu, wd, post2, fin):
    n_tok = x1.shape[0]
    tm = TOKEN_TILE
    tok = lambda w: pl.BlockSpec((tm, w), lambda i: (i, 0))
    vec = _const_spec((1, D_MODEL))
    return pl.pallas_call(
        _out_ffn_kernel,
        grid=(n_tok // tm,),
        in_specs=[tok(D_MODEL), tok(D_A), tok(D_B), _const_spec(wout.shape), vec, vec, _const_spec(wgu.shape),
                  _const_spec(wd.shape), vec, vec],
        out_specs=tok(D_MODEL),
        out_shape=jax.ShapeDtypeStruct((n_tok, D_MODEL), F32),
        compiler_params=pltpu.CompilerParams(dimension_semantics=("parallel",), vmem_limit_bytes=VMEM_LIMIT),
        name="out_ffn",
    )(x1, oa, ob, wout, mixpost, pre2, wgu, wd, post2, fin)


def _rope_tables(seq):
    half = HEAD_DIM // 2
    inv = ROPE_THETA ** (-jnp.arange(0, HEAD_DIM, 2, dtype=F32) / HEAD_DIM)
    ang = jnp.arange(seq, dtype=F32)[:, None] * inv[None, :]
    cos, sin = jnp.cos(ang), jnp.sin(ang)
    reps = LANES // HEAD_DIM
    return jnp.tile(jnp.concatenate([cos, cos], axis=1), (1, reps)), jnp.tile(
        jnp.concatenate([-sin, sin], axis=1), (1, reps))


def _layer(x, p):
    b, seq, _ = x.shape
    cos_t, sin_t = _rope_tables(seq)
    x1, qa, ka, va, qb, kb, vb = _ffn_in(x.reshape(b * seq, D_MODEL), seq, cos_t, sin_t, p["ffn1_pre"], p["ffn1_w_gu"],
                                         p["ffn1_w_down"], p["ffn1_post"], p["mix_pre"], p["w_in"])
    r3 = lambda a: a.reshape(b, seq, a.shape[-1])
    oa = _attn_win(r3(qa), r3(ka), r3(va), p["sink_a"], p["out_norm_a"])
    ob = _attn_nbr(r3(qb), r3(kb), r3(vb), p["nbr_bias"], p["out_norm_b"])
    y = _out_ffn(x1, oa.reshape(b * seq, D_A), ob.reshape(b * seq, D_B), p["w_out"], p["mix_post"], p["ffn2_pre"],
                 p["ffn2_w_gu"], p["ffn2_w_down"], p["ffn2_post"], p["final_norm"])
    return y.reshape(b, seq, D_MODEL)


def kernel(x_prompt, x_sample, ffn1_pre, ffn1_w_gu, ffn1_w_down, ffn1_post, mix_pre, w_in, sink_a, rpb_b, out_norm_a,
           out_norm_b, w_out, mix_post, ffn2_pre, ffn2_w_gu, ffn2_w_down, ffn2_post, final_norm):
    y_prompt, y_sample = x_prompt, x_sample
    half = N_HEADS_A // 2
    head_order = [h for t in range(half) for h in (t, t + half)]
    perm = np.concatenate([np.arange(h * HEAD_DIM, (h + 1) * HEAD_DIM) for h in head_order])
    in_cols = np.concatenate([perm, np.arange(D_A, D_IN)])
    out_rows = np.concatenate([perm, np.arange(D_A, D_A + D_B)])
    for l in range(ffn1_pre.shape[0]):
        row = lambda a: a[l].astype(F32).reshape(1, -1)
        p = {
            "ffn1_pre": row(ffn1_pre), "ffn1_post": row(ffn1_post), "mix_pre": row(mix_pre),
            "out_norm_a": row(out_norm_a)[:, perm], "out_norm_b": row(out_norm_b), "mix_post": row(mix_post),
            "ffn2_pre": row(ffn2_pre), "ffn2_post": row(ffn2_post), "final_norm": row(final_norm),
            "ffn1_w_gu": ffn1_w_gu[l].astype(BF16), "ffn1_w_down": ffn1_w_down[l].astype(BF16),
            "ffn2_w_gu": ffn2_w_gu[l].astype(BF16), "ffn2_w_down": ffn2_w_down[l].astype(BF16),
            "w_in": w_in[l][:, in_cols].astype(BF16), "w_out": w_out[l][out_rows, :].astype(BF16),
            "sink_a": sink_a[l].astype(F32), "nbr_bias": _nbr_bias_table(rpb_b[l]),
        }
        y_prompt = _layer(y_prompt, p)
        y_sample = _layer(y_sample, p)
    return (y_prompt, y_sample)
```

```python
import functools

import jax
import jax.numpy as jnp
import numpy as np
from jax import lax
from jax.experimental import pallas as pl
from jax.experimental.pallas import tpu as pltpu

D_MODEL = 1024
HEAD_DIM = 64
N_HEADS_A = 8
N_KV_A = 2
N_HEADS_B = 8
WIN = 128
BLK = 128
GRID_W = 64
NA_ROWS = 8
NA_COLS = 16
D_FF = 2816
ROPE_THETA = 10000.0
EPS = 1e-6
D_A = N_HEADS_A * HEAD_DIM
D_KV_A = N_KV_A * HEAD_DIM
D_B = N_HEADS_B * HEAD_DIM
D_IN = D_A + 2 * D_KV_A + 3 * D_B
D_ROPE = D_A + D_KV_A
NEG = -1e30
SCALE = HEAD_DIM ** -0.5

LANES = 128
MXU_DIM = 256
FFN_CHUNKS = ((0, 6 * MXU_DIM), (6 * MXU_DIM, 5 * MXU_DIM))
TOKEN_TILE = 512
TOKEN_SUBTILES = 2
WIN_Q_TILE = 512
NBR_ROWS_PER_STEP = 8
NBR_ROW_GROUP = 8
NBR_KEYS = NA_ROWS * GRID_W
NBR_COL_BLOCKS = GRID_W // NA_COLS
NBR_WINDOW_TILES = 2
SUBLANES = 8
VMEM_LIMIT = 56 * 1024 * 1024

BF16 = jnp.bfloat16
F32 = jnp.float32

assert sum(w for _, w in FFN_CHUNKS) == D_FF
assert NA_ROWS * NA_COLS == LANES


def _dot(a, b):
    return jnp.dot(a, b, preferred_element_type=F32)


def _dot_nt(a, b):
    return lax.dot_general(a, b, (((1,), (1,)), ((), ())), preferred_element_type=F32)


def _rms(x, g):
    return x * lax.rsqrt(jnp.mean(x * x, axis=-1, keepdims=True) + EPS) * g


def _subtiles(ref):
    rows = ref.shape[0] // TOKEN_SUBTILES
    return [ref[k * rows:(k + 1) * rows, :] for k in range(TOKEN_SUBTILES)]


def _store_subtiles(ref, values):
    rows = ref.shape[0] // TOKEN_SUBTILES
    for k, v in enumerate(values):
        ref[k * rows:(k + 1) * rows, :] = v.astype(ref.dtype)


def _swiglu(xns, wgu_ref, wd_ref):
    accs = [None] * len(xns)
    for lo, width in FFN_CHUNKS:
        hs = []
        for xn in xns:
            g = _dot(xn, wgu_ref[:, lo:lo + width])
            u = _dot(xn, wgu_ref[:, D_FF + lo:D_FF + lo + width])
            hs.append((g * jax.nn.sigmoid(g) * u).astype(BF16))
        for k, h in enumerate(hs):
            part = _dot(h, wd_ref[lo:lo + width, :])
            accs[k] = part if accs[k] is None else accs[k] + part
    return accs


def _rope_qkv(proj, cos, sin, first_half):
    roped = []
    for t in range(D_ROPE // LANES):
        xt = proj[:, t * LANES:(t + 1) * LANES]
        partner = jnp.where(first_half,
                            pltpu.roll(xt, LANES - HEAD_DIM // 2, 1),
                            pltpu.roll(xt, HEAD_DIM // 2, 1))
        roped.append(xt * cos + partner * sin)
    o = D_ROPE
    va = proj[:, o:o + D_KV_A]
    o += D_KV_A
    qb = proj[:, o:o + D_B] * SCALE
    o += D_B
    kb = proj[:, o:o + D_B]
    o += D_B
    vb = proj[:, o:o + D_B]
    return jnp.concatenate(roped[:D_A // LANES], axis=1) * SCALE, roped[D_A // LANES], va, qb, kb, vb


def _ffn_in_kernel(x_ref, cos_ref, sin_ref, pre1_ref, wgu_ref, wd_ref, post1_ref, mixpre_ref, win_ref,
                   x1_ref, qa_ref, ka_ref, va_ref, qb_ref, kb_ref, vb_ref):
    xs = _subtiles(x_ref)
    fs = _swiglu([_rms(x, pre1_ref[...]).astype(BF16) for x in xs], wgu_ref, wd_ref)
    x1s = [x + 0.5 * _rms(f, post1_ref[...]) for x, f in zip(xs, fs)]
    _store_subtiles(x1_ref, x1s)
    projs = [_dot(_rms(x1, mixpre_ref[...]).astype(BF16), win_ref[...]) for x1 in x1s]

    lane = lax.broadcasted_iota(jnp.int32, (xs[0].shape[0], LANES), 1)
    first_half = (lane % HEAD_DIM) < (HEAD_DIM // 2)
    outs = [_rope_qkv(proj, cos, sin, first_half)
            for proj, cos, sin in zip(projs, _subtiles(cos_ref), _subtiles(sin_ref))]
    qa, ka, va, qb, kb, vb = zip(*outs)
    for ref, values in zip((qa_ref, ka_ref, va_ref, qb_ref), (qa, ka, va, qb)):
        _store_subtiles(ref, values)
    _store_col_blocked(kb_ref, kb)
    _store_col_blocked(vb_ref, vb)


def _store_col_blocked(ref, values):
    rows_per_group = values[0].shape[0] // GRID_W
    for k, v in enumerate(values):
        v = v.astype(ref.dtype)
        for r in range(rows_per_group):
            dst = (k * rows_per_group + r) * NA_COLS
            for cb in range(NBR_COL_BLOCKS):
                src = r * GRID_W + cb * NA_COLS
                ref[0, cb, dst:dst + NA_COLS, :] = v[src:src + NA_COLS, :]


def _const_spec(shape):
    return pl.BlockSpec(shape, lambda *_: (0,) * len(shape), pipeline_mode=pl.Buffered(1))


def _ffn_in(x2d, seq, cos_t, sin_t, pre1, wgu, wd, post1, mixpre, win):
    n_tok = x2d.shape[0]
    tm = TOKEN_TILE
    tiles_per_seq = seq // tm
    tok = lambda w: pl.BlockSpec((tm, w), lambda i: (i, 0))
    rope_spec = pl.BlockSpec((tm, LANES), lambda i: (i % tiles_per_seq, 0))
    vec = _const_spec((1, D_MODEL))
    out_widths = (D_A, D_KV_A, D_KV_A, D_B)
    blocked_rows = tm // NBR_COL_BLOCKS
    blocked_spec = pl.BlockSpec((1, NBR_COL_BLOCKS, blocked_rows, D_B),
                                lambda i: (i // tiles_per_seq, 0, i % tiles_per_seq, 0))
    blocked_shape = jax.ShapeDtypeStruct((n_tok // seq, NBR_COL_BLOCKS, seq // NBR_COL_BLOCKS, D_B), BF16)
    return pl.pallas_call(
        _ffn_in_kernel,
        grid=(n_tok // tm,),
        in_specs=[tok(D_MODEL), rope_spec, rope_spec, vec, _const_spec(wgu.shape), _const_spec(wd.shape), vec, vec,
                  _const_spec(win.shape)],
        out_specs=[tok(D_MODEL)] + [tok(w) for w in out_widths] + [blocked_spec, blocked_spec],
        out_shape=[jax.ShapeDtypeStruct((n_tok, D_MODEL), F32)]
        + [jax.ShapeDtypeStruct((n_tok, w), BF16) for w in out_widths] + [blocked_shape, blocked_shape],
        compiler_params=pltpu.CompilerParams(dimension_semantics=("parallel",), vmem_limit_bytes=VMEM_LIMIT),
        name="ffn_in",
    )(x2d, cos_t, sin_t, pre1, wgu, wd, post1, mixpre, win)


def _attn_win_kernel(sink_ref, q_ref, k_ref, v_ref, g_ref, o_ref, *, seq):
    i = pl.program_id(1)
    nk = BLK + 2 * WIN
    n_tiles = D_A // LANES
    n_blocks = WIN_Q_TILE // BLK
    lower = lax.broadcasted_iota(jnp.int32, (BLK, LANES), 1) < HEAD_DIM
    rel = lax.broadcasted_iota(jnp.int32, (BLK, nk), 1) - lax.broadcasted_iota(jnp.int32, (BLK, nk), 0)
    block_of_row = lax.broadcasted_iota(jnp.int32, (2 * n_tiles * BLK, 1), 0) // BLK
    sink = jnp.zeros((2 * n_tiles * BLK, 1), F32)
    for t in range(n_tiles):
        for u in range(2):
            sink = jnp.where(block_of_row == 2 * t + u, sink_ref[t + u * n_tiles], sink)

    scores, v_windows = [], []
    for j in range(n_blocks):
        q0 = i * WIN_Q_TILE + j * BLK
        start = pl.multiple_of(jnp.clip(q0 - WIN, 0, seq - nk), BLK)
        mask = jnp.where(jnp.abs(rel - (q0 - start)) <= WIN, 0.0, NEG)
        stacked = []
        for t in range(n_tiles):
            qt = q_ref[0, j * BLK:(j + 1) * BLK, t * LANES:(t + 1) * LANES]
            stacked += [jnp.where(lower, qt, jnp.zeros_like(qt)), jnp.where(lower, jnp.zeros_like(qt), qt)]
        s = _dot_nt(jnp.concatenate(stacked, axis=0), k_ref[0, pl.ds(start, nk), :])
        scores.append(s + jnp.concatenate([mask] * (2 * n_tiles), axis=0))
        v_windows.append(v_ref[0, pl.ds(start, nk), :])

    for j in range(n_blocks):
        s = scores[j]
        m = jnp.maximum(jnp.max(s, axis=-1, keepdims=True), sink)
        p = jnp.exp(s - m)
        denom = jnp.sum(p, axis=-1, keepdims=True) + jnp.exp(sink - m)
        o = _dot(p.astype(BF16), v_windows[j]) * (1.0 / denom)
        tiles = [jnp.where(lower, o[2 * t * BLK:(2 * t + 1) * BLK], o[(2 * t + 1) * BLK:(2 * t + 2) * BLK])
                 for t in range(n_tiles)]
        oa = jnp.concatenate(tiles, axis=1)
        o_ref[0, j * BLK:(j + 1) * BLK, :] = _rms(oa, g_ref[...]).astype(BF16)


def _attn_win(qa, ka, va, sink, gain):
    b, seq, _ = qa.shape
    kv_spec = pl.BlockSpec((1, seq, D_KV_A), lambda bi, i: (bi, 0, 0), pipeline_mode=pl.Buffered(1))
    q_spec = pl.BlockSpec((1, WIN_Q_TILE, D_A), lambda bi, i: (bi, i, 0))
    return pl.pallas_call(
        functools.partial(_attn_win_kernel, seq=seq),
        grid=(b, seq // WIN_Q_TILE),
        in_specs=[pl.BlockSpec(memory_space=pltpu.SMEM), q_spec, kv_spec, kv_spec, _const_spec((1, D_A))],
        out_specs=q_spec,
        out_shape=jax.ShapeDtypeStruct((b, seq, D_A), BF16),
        compiler_params=pltpu.CompilerParams(dimension_semantics=("parallel", "arbitrary"),
                                             vmem_limit_bytes=VMEM_LIMIT),
        name="attn_win",
    )(sink, qa, ka, va, gain)


def _attn_nbr_kernel(q_ref, k_ref, v_ref, bias_ref, g_ref, o_ref, *, grid_rows):
    i = pl.program_id(1)
    n_tiles = D_B // LANES
    blocks = _nbr_blocks()
    lower = {n: lax.broadcasted_iota(jnp.int32, (n, LANES), 1) < HEAD_DIM
             for n in {GRID_W} | {hi - lo for lo, hi, _ in blocks}}

    def window(ref, start, ts):
        return jnp.concatenate([ref[0, cb, pl.ds(start, LANES), ts] for cb in range(NBR_COL_BLOCKS)], axis=0)

    def row_group(gi, carry):
        rows = []
        for rr in range(NBR_ROW_GROUP):
            row = gi * NBR_ROW_GROUP + rr
            r = i * NBR_ROWS_PER_STEP + row
            r0 = jnp.clip(r - NA_ROWS // 2, 0, grid_rows - NA_ROWS)
            shift = r0 - r + NA_ROWS - 1
            wstart = pl.multiple_of(r0 * NA_COLS, NA_COLS)
            qoff = pl.multiple_of(row * GRID_W, GRID_W)
            per_tile = []
            for t in range(n_tiles):
                ts = slice(t * LANES, (t + 1) * LANES)
                qt = q_ref[0, pl.ds(qoff, GRID_W), ts].astype(F32)
                q_even = jnp.where(lower[GRID_W], qt, 0.0)
                q_odd = jnp.where(lower[GRID_W], 0.0, qt)
                qs = jnp.concatenate([part[lo:hi] for lo, hi, _ in blocks for part in (q_even, q_odd)], axis=0)
                s = _dot_nt(qs.astype(BF16), window(k_ref, wstart, ts))
                per_tile.append((s, window(v_ref, wstart, ts)))
            rows.append((shift, qoff, per_tile))
        for shift, qoff, per_tile in rows:
            tiles = []
            for t, (s, v_window) in enumerate(per_tile):
                seen = jnp.concatenate([s[2 * lo:2 * hi, w * LANES:(w + NBR_WINDOW_TILES) * LANES]
                                        for lo, hi, w in blocks], axis=0) + bias_ref[shift, t]
                p = jnp.exp(seen - jnp.max(seen, axis=-1, keepdims=True))
                inv_denom = 1.0 / jnp.sum(p, axis=-1, keepdims=True)
                p = p.astype(BF16)
                zeros = lambda rows_, n: [jnp.zeros((rows_, n * LANES), BF16)] if n else []
                p_full = jnp.concatenate(
                    [jnp.concatenate(zeros(2 * (hi - lo), w) + [p[2 * lo:2 * hi]]
                                     + zeros(2 * (hi - lo), NBR_COL_BLOCKS - NBR_WINDOW_TILES - w), axis=1)
                     for lo, hi, w in blocks], axis=0)
                o = _dot(p_full, v_window) * inv_denom
                tiles.append(jnp.concatenate(
                    [jnp.where(lower[hi - lo], o[2 * lo:lo + hi], o[lo + hi:2 * hi]) for lo, hi, _ in blocks],
                    axis=0))
            ob = jnp.concatenate(tiles, axis=1)
            o_ref[0, pl.ds(qoff, GRID_W), :] = _rms(ob, g_ref[...]).astype(BF16)
        return carry

    lax.fori_loop(0, NBR_ROWS_PER_STEP // NBR_ROW_GROUP, row_group, 0)


def _window_start_col(qc):
    return np.clip(qc - NA_COLS // 2, 0, GRID_W - NA_COLS)


def _nbr_blocks():
    blocks = []
    for lo in range(0, GRID_W, SUBLANES):
        c0 = _window_start_col(np.arange(lo, lo + SUBLANES))
        first, last = int(c0.min()) // NA_COLS, (int(c0.max()) + NA_COLS - 1) // NA_COLS
        assert last - first < NBR_WINDOW_TILES
        cb = min(first, NBR_COL_BLOCKS - NBR_WINDOW_TILES)
        if blocks and blocks[-1][2] == cb:
            blocks[-1] = (blocks[-1][0], lo + SUBLANES, cb)
        else:
            blocks.append((lo, lo + SUBLANES, cb))
    return blocks


def _nbr_bias_table(rpb):
    qc = np.arange(GRID_W)[:, None]
    kc = np.arange(GRID_W)[None, :]
    c0 = _window_start_col(qc)
    valid = (kc >= c0) & (kc < c0 + NA_COLS)
    col_idx = np.clip(kc - qc, -(NA_COLS - 1), NA_COLS - 1) + NA_COLS - 1
    toeplitz = jnp.where(valid[None, None], rpb.astype(F32)[:, :, col_idx], NEG)
    row_idx = np.arange(NA_ROWS)[:, None] + np.arange(NA_ROWS)[None, :]
    slabs = toeplitz[:, row_idx]
    slabs = slabs.reshape(N_HEADS_B, NA_ROWS, NA_ROWS, GRID_W, NBR_COL_BLOCKS, NA_COLS)
    slabs = slabs.transpose(1, 0, 3, 4, 2, 5).reshape(NA_ROWS, N_HEADS_B // 2, 2, GRID_W, NBR_COL_BLOCKS, LANES)
    parts = []
    for lo, hi, cb in _nbr_blocks():
        seen = slabs[:, :, :, lo:hi, cb:cb + NBR_WINDOW_TILES]
        parts.append(seen.reshape(NA_ROWS, N_HEADS_B // 2, 2 * (hi - lo), NBR_WINDOW_TILES * LANES))
    return jnp.concatenate(parts, axis=2)


def _attn_nbr(qb, kb, vb, bias, gain):
    b, seq, _ = qb.shape
    grid_rows = seq // GRID_W
    tq = NBR_ROWS_PER_STEP * GRID_W
    kv_spec = pl.BlockSpec((1,) + kb.shape[1:], lambda bi, i: (bi, 0, 0, 0), pipeline_mode=pl.Buffered(1))
    q_spec = pl.BlockSpec((1, tq, D_B), lambda bi, i: (bi, i, 0))
    return pl.pallas_call(
        functools.partial(_attn_nbr_kernel, grid_rows=grid_rows),
        grid=(b, seq // tq),
        in_specs=[q_spec, kv_spec, kv_spec, _const_spec(bias.shape), _const_spec((1, D_B))],
        out_specs=q_spec,
        out_shape=jax.ShapeDtypeStruct((b, seq, D_B), BF16),
        compiler_params=pltpu.CompilerParams(dimension_semantics=("parallel", "arbitrary"),
                                             vmem_limit_bytes=VMEM_LIMIT),
        name="attn_nbr",
    )(qb, kb, vb, bias, gain)


def _out_ffn_kernel(x1_ref, oa_ref, ob_ref, wout_ref, mixpost_ref, pre2_ref, wgu_ref, wd_ref, post2_ref, fin_ref,
                    y_ref):
    mixes = [_dot(jnp.concatenate([oa, ob], axis=1), wout_ref[...])
             for oa, ob in zip(_subtiles(oa_ref), _subtiles(ob_ref))]
    x2s = [x1 + _rms(mix, mixpost_ref[...]) for x1, mix in zip(_subtiles(x1_ref), mixes)]
    fs = _swiglu([_rms(x2, pre2_ref[...]).astype(BF16) for x2 in x2s], wgu_ref, wd_ref)
    x3s = [x2 + 0.5 * _rms(f, post2_ref[...]) for x2, f in zip(x2s, fs)]
    _store_subtiles(y_ref, [_rms(x3, fin_ref[...]) for x3 in x3s])


def _out_ffn(x1, oa, ob, wout, mixpost, pre2, wgu, wd, post2, fin):
    n_tok = x1.shape[0]
    tm = TOKEN_TILE
    tok = lambda w: pl.BlockSpec((tm, w), lambda i: (i, 0))
    vec = _const_spec((1, D_MODEL))
    return pl.pallas_call(
        _out_ffn_kernel,
        grid=(n_tok // tm,),
        in_specs=[tok(D_MODEL), tok(D_A), tok(D_B), _const_spec(wout.shape), vec, vec, _const_spec(wgu.shape),
                  _const_spec(wd.shape), vec, vec],
        out_specs=tok(D_MODEL),
        out_shape=jax.ShapeDtypeStruct((n_tok, D_MODEL), F32),
        compiler_params=pltpu.CompilerParams(dimension_semantics=("parallel",), vmem_limit_bytes=VMEM_LIMIT),
        name="out_ffn",
    )(x1, oa, ob, wout, mixpost, pre2, wgu, wd, post2, fin)


def _rope_tables(seq):
    half = HEAD_DIM // 2
    inv = ROPE_THETA ** (-jnp.arange(0, HEAD_DIM, 2, dtype=F32) / HEAD_DIM)
    ang = jnp.arange(seq, dtype=F32)[:, None] * inv[None, :]
    cos, sin = jnp.cos(ang), jnp.sin(ang)
    reps = LANES // HEAD_DIM
    return jnp.tile(jnp.concatenate([cos, cos], axis=1), (1, reps)), jnp.tile(
        jnp.concatenate([-sin, sin], axis=1), (1, reps))


def _layer(x, p):
    b, seq, _ = x.shape
    cos_t, sin_t = _rope_tables(seq)
    x1, qa, ka, va, qb, kb, vb = _ffn_in(x.reshape(b * seq, D_MODEL), seq, cos_t, sin_t, p["ffn1_pre"], p["ffn1_w_gu"],
                                         p["ffn1_w_down"], p["ffn1_post"], p["mix_pre"], p["w_in"])
    r3 = lambda a: a.reshape(b, seq, a.shape[-1])
    oa = _attn_win(r3(qa), r3(ka), r3(va), p["sink_a"], p["out_norm_a"])
    ob = _attn_nbr(r3(qb), kb, vb, p["nbr_bias"], p["out_norm_b"])
    y = _out_ffn(x1, oa.reshape(b * seq, D_A), ob.reshape(b * seq, D_B), p["w_out"], p["mix_post"], p["ffn2_pre"],
                 p["ffn2_w_gu"], p["ffn2_w_down"], p["ffn2_post"], p["final_norm"])
    return y.reshape(b, seq, D_MODEL)


def kernel(x_prompt, x_sample, ffn1_pre, ffn1_w_gu, ffn1_w_down, ffn1_post, mix_pre, w_in, sink_a, rpb_b, out_norm_a,
           out_norm_b, w_out, mix_post, ffn2_pre, ffn2_w_gu, ffn2_w_down, ffn2_post, final_norm):
    y_prompt, y_sample = x_prompt, x_sample
    half = N_HEADS_A // 2
    head_order = [h for t in range(half) for h in (t, t + half)]
    perm = np.concatenate([np.arange(h * HEAD_DIM, (h + 1) * HEAD_DIM) for h in head_order])
    in_cols = np.concatenate([perm, np.arange(D_A, D_IN)])
    out_rows = np.concatenate([perm, np.arange(D_A, D_A + D_B)])
    for l in range(ffn1_pre.shape[0]):
        row = lambda a: a[l].astype(F32).reshape(1, -1)
        p = {
            "ffn1_pre": row(ffn1_pre), "ffn1_post": row(ffn1_post), "mix_pre": row(mix_pre),
            "out_norm_a": row(out_norm_a)[:, perm], "out_norm_b": row(out_norm_b), "mix_post": row(mix_post),
            "ffn2_pre": row(ffn2_pre), "ffn2_post": row(ffn2_post), "final_norm": row(final_norm),
            "ffn1_w_gu": ffn1_w_gu[l].astype(BF16), "ffn1_w_down": ffn1_w_down[l].astype(BF16),
            "ffn2_w_gu": ffn2_w_gu[l].astype(BF16), "ffn2_w_down": ffn2_w_down[l].astype(BF16),
            "w_in": w_in[l][:, in_cols].astype(BF16), "w_out": w_out[l][out_rows, :].astype(BF16),
            "sink_a": sink_a[l].astype(F32), "nbr_bias": _nbr_bias_table(rpb_b[l]),
        }
        y_prompt = _layer(y_prompt, p)
        y_sample = _layer(y_sample, p)
    return (y_prompt, y_sample)
```

```python
import functools

import jax
import jax.numpy as jnp
import numpy as np
from jax import lax
from jax.experimental import pallas as pl
from jax.experimental.pallas import tpu as pltpu

D_MODEL = 1024
HEAD_DIM = 64
N_HEADS_A = 8
N_KV_A = 2
N_HEADS_B = 8
WIN = 128
BLK = 128
GRID_W = 64
NA_ROWS = 8
NA_COLS = 16
D_FF = 2816
ROPE_THETA = 10000.0
EPS = 1e-6
D_A = N_HEADS_A * HEAD_DIM
D_KV_A = N_KV_A * HEAD_DIM
D_B = N_HEADS_B * HEAD_DIM
D_IN = D_A + 2 * D_KV_A + 3 * D_B
D_ROPE = D_A + D_KV_A
NEG = -1e30
SCALE = HEAD_DIM ** -0.5
LOG2E = 1.4426950408889634

LANES = 128
MXU_DIM = 256
FFN_CHUNKS = ((0, 6 * MXU_DIM), (6 * MXU_DIM, 5 * MXU_DIM))
TOKEN_TILE = 512
TOKEN_SUBTILE_ROWS = (192, 192, 128)
WIN_Q_TILE = 1024
NBR_ROWS_PER_STEP = 8
NBR_ROW_GROUP = 8
NBR_KEYS = NA_ROWS * GRID_W
NBR_COL_BLOCKS = GRID_W // NA_COLS
NBR_WINDOW_TILES = 2
SUBLANES = 8
VMEM_LIMIT = 56 * 1024 * 1024

BF16 = jnp.bfloat16
F32 = jnp.float32

assert sum(w for _, w in FFN_CHUNKS) == D_FF
assert NA_ROWS * NA_COLS == LANES


def _dot(a, b):
    return jnp.dot(a, b, preferred_element_type=F32)


def _dot_nt(a, b):
    return lax.dot_general(a, b, (((1,), (1,)), ((), ())), preferred_element_type=F32)


def _rms(x, g):
    return x * lax.rsqrt(jnp.mean(x * x, axis=-1, keepdims=True) + EPS) * g


def _subtile_bounds():
    edges = np.cumsum((0,) + TOKEN_SUBTILE_ROWS)
    return list(zip(edges[:-1], edges[1:]))


def _subtiles(ref):
    return [ref[lo:hi, :] for lo, hi in _subtile_bounds()]


def _store_subtiles(ref, values):
    for (lo, hi), v in zip(_subtile_bounds(), values):
        ref[lo:hi, :] = v.astype(ref.dtype)


def _swiglu(xns, wgu_ref, wd_ref):
    accs = [None] * len(xns)
    for lo, width in FFN_CHUNKS:
        hs = []
        for xn in xns:
            g = _dot(xn, wgu_ref[:, lo:lo + width])
            u = _dot(xn, wgu_ref[:, D_FF + lo:D_FF + lo + width])
            hs.append((g * jax.nn.sigmoid(g) * u).astype(BF16))
        for k, h in enumerate(hs):
            part = _dot(h, wd_ref[lo:lo + width, :])
            accs[k] = part if accs[k] is None else accs[k] + part
    return accs


def _rope_qkv(proj, cos, sin):
    lane = lax.broadcasted_iota(jnp.int32, cos.shape, 1)
    first_half = (lane % HEAD_DIM) < (HEAD_DIM // 2)
    roped = []
    for t in range(D_ROPE // LANES):
        xt = proj[:, t * LANES:(t + 1) * LANES]
        partner = jnp.where(first_half,
                            pltpu.roll(xt, LANES - HEAD_DIM // 2, 1),
                            pltpu.roll(xt, HEAD_DIM // 2, 1))
        roped.append(xt * cos + partner * sin)
    o = D_ROPE
    va = proj[:, o:o + D_KV_A]
    o += D_KV_A
    qb = proj[:, o:o + D_B] * SCALE
    o += D_B
    kb = proj[:, o:o + D_B]
    o += D_B
    vb = proj[:, o:o + D_B]
    return jnp.concatenate(roped[:D_A // LANES], axis=1) * (SCALE * LOG2E), roped[D_A // LANES], va, qb, kb, vb


def _ffn_in_kernel(x_ref, cos_ref, sin_ref, pre1_ref, wgu_ref, wd_ref, post1_ref, mixpre_ref, win_ref,
                   x1_ref, qa_ref, ka_ref, va_ref, qb_ref, kb_ref, vb_ref):
    xs = _subtiles(x_ref)
    fs = _swiglu([_rms(x, pre1_ref[...]).astype(BF16) for x in xs], wgu_ref, wd_ref)
    x1s = [x + 0.5 * _rms(f, post1_ref[...]) for x, f in zip(xs, fs)]
    _store_subtiles(x1_ref, x1s)
    projs = [_dot(_rms(x1, mixpre_ref[...]).astype(BF16), win_ref[...]) for x1 in x1s]

    outs = [_rope_qkv(proj, cos, sin) for proj, cos, sin in zip(projs, _subtiles(cos_ref), _subtiles(sin_ref))]
    qa, ka, va, qb, kb, vb = zip(*outs)
    for ref, values in zip((qa_ref, ka_ref, va_ref, qb_ref), (qa, ka, va, qb)):
        _store_subtiles(ref, values)
    _store_col_blocked(kb_ref, kb)
    _store_col_blocked(vb_ref, vb)


def _store_col_blocked(ref, values):
    grid_row = 0
    for v in values:
        v = v.astype(ref.dtype)
        for r in range(v.shape[0] // GRID_W):
            dst = grid_row * NA_COLS
            for cb in range(NBR_COL_BLOCKS):
                src = r * GRID_W + cb * NA_COLS
                ref[0, cb, dst:dst + NA_COLS, :] = v[src:src + NA_COLS, :]
            grid_row += 1


def _const_spec(shape):
    return pl.BlockSpec(shape, lambda *_: (0,) * len(shape), pipeline_mode=pl.Buffered(1))


def _ffn_in(x2d, seq, cos_t, sin_t, pre1, wgu, wd, post1, mixpre, win):
    n_tok = x2d.shape[0]
    tm = TOKEN_TILE
    tiles_per_seq = seq // tm
    tok = lambda w: pl.BlockSpec((tm, w), lambda i: (i, 0))
    rope_spec = pl.BlockSpec((tm, LANES), lambda i: (i % tiles_per_seq, 0))
    vec = _const_spec((1, D_MODEL))
    out_widths = (D_A, D_KV_A, D_KV_A, D_B)
    blocked_rows = tm // NBR_COL_BLOCKS
    blocked_spec = pl.BlockSpec((1, NBR_COL_BLOCKS, blocked_rows, D_B),
                                lambda i: (i // tiles_per_seq, 0, i % tiles_per_seq, 0))
    blocked_shape = jax.ShapeDtypeStruct((n_tok // seq, NBR_COL_BLOCKS, seq // NBR_COL_BLOCKS, D_B), BF16)
    return pl.pallas_call(
        _ffn_in_kernel,
        grid=(n_tok // tm,),
        in_specs=[tok(D_MODEL), rope_spec, rope_spec, vec, _const_spec(wgu.shape), _const_spec(wd.shape), vec, vec,
                  _const_spec(win.shape)],
        out_specs=[tok(D_MODEL)] + [tok(w) for w in out_widths] + [blocked_spec, blocked_spec],
        out_shape=[jax.ShapeDtypeStruct((n_tok, D_MODEL), F32)]
        + [jax.ShapeDtypeStruct((n_tok, w), BF16) for w in out_widths] + [blocked_shape, blocked_shape],
        compiler_params=pltpu.CompilerParams(dimension_semantics=("parallel",), vmem_limit_bytes=VMEM_LIMIT),
        name="ffn_in",
    )(x2d, cos_t, sin_t, pre1, wgu, wd, post1, mixpre, win)


def _attn_win_kernel(sink_ref, q_ref, k_ref, v_ref, g_ref, o_ref, *, seq):
    i = pl.program_id(1)
    nk = BLK + 2 * WIN
    n_tiles = D_A // LANES
    n_blocks = WIN_Q_TILE // BLK
    lower = lax.broadcasted_iota(jnp.int32, (BLK, LANES), 1) < HEAD_DIM
    rel = lax.broadcasted_iota(jnp.int32, (BLK, nk), 1) - lax.broadcasted_iota(jnp.int32, (BLK, nk), 0)
    block_of_row = lax.broadcasted_iota(jnp.int32, (2 * n_tiles * BLK, 1), 0) // BLK
    sink = jnp.zeros((2 * n_tiles * BLK, 1), F32)
    for t in range(n_tiles):
        for u in range(2):
            sink = jnp.where(block_of_row == 2 * t + u, sink_ref[t + u * n_tiles] * LOG2E, sink)

    def window_start(j):
        q0 = i * WIN_Q_TILE + j * BLK
        start = pl.multiple_of(jnp.clip(q0 - WIN, 0, seq - nk), BLK)
        return q0, start

    scores = []
    for j in range(n_blocks):
        _, start = window_start(j)
        stacked = []
        for t in range(n_tiles):
            qt = q_ref[0, j * BLK:(j + 1) * BLK, t * LANES:(t + 1) * LANES]
            stacked += [jnp.where(lower, qt, jnp.zeros_like(qt)), jnp.where(lower, jnp.zeros_like(qt), qt)]
        scores.append(_dot_nt(jnp.concatenate(stacked, axis=0), k_ref[0, pl.ds(start, nk), :]))

    for j in range(n_blocks):
        q0, start = window_start(j)
        mask = jnp.where(jnp.abs(rel - (q0 - start)) <= WIN, 0.0, NEG)
        s = scores[j] + jnp.concatenate([mask] * (2 * n_tiles), axis=0)
        m = jnp.maximum(jnp.max(s, axis=-1, keepdims=True), sink)
        p = jnp.exp2(s - m)
        denom = jnp.sum(p, axis=-1, keepdims=True) + jnp.exp2(sink - m)
        o = _dot(p.astype(BF16), v_ref[0, pl.ds(start, nk), :]) * (1.0 / denom)
        tiles = [jnp.where(lower, o[2 * t * BLK:(2 * t + 1) * BLK], o[(2 * t + 1) * BLK:(2 * t + 2) * BLK])
                 for t in range(n_tiles)]
        oa = jnp.concatenate(tiles, axis=1)
        o_ref[0, j * BLK:(j + 1) * BLK, :] = _rms(oa, g_ref[...]).astype(BF16)


def _attn_win(qa, ka, va, sink, gain):
    b, seq, _ = qa.shape
    kv_spec = pl.BlockSpec((1, seq, D_KV_A), lambda bi, i: (bi, 0, 0))
    q_spec = pl.BlockSpec((1, WIN_Q_TILE, D_A), lambda bi, i: (bi, i, 0))
    return pl.pallas_call(
        functools.partial(_attn_win_kernel, seq=seq),
        grid=(b, seq // WIN_Q_TILE),
        in_specs=[pl.BlockSpec(memory_space=pltpu.SMEM), q_spec, kv_spec, kv_spec, _const_spec((1, D_A))],
        out_specs=q_spec,
        out_shape=jax.ShapeDtypeStruct((b, seq, D_A), BF16),
        compiler_params=pltpu.CompilerParams(dimension_semantics=("parallel", "arbitrary"),
                                             vmem_limit_bytes=VMEM_LIMIT),
        name="attn_win",
    )(sink, qa, ka, va, gain)


def _attn_nbr_kernel(q_ref, k_ref, v_ref, bias_ref, g_ref, o_ref, *, grid_rows):
    i = pl.program_id(1)
    n_tiles = D_B // LANES
    blocks = _nbr_blocks()
    lower = {n: lax.broadcasted_iota(jnp.int32, (n, LANES), 1) < HEAD_DIM
             for n in {GRID_W} | {hi - lo for lo, hi, _ in blocks}}

    def window(ref, start, ts):
        return jnp.concatenate([ref[0, cb, pl.ds(start, LANES), ts] for cb in range(NBR_COL_BLOCKS)], axis=0)

    def row_group(gi, carry):
        rows = []
        for rr in range(NBR_ROW_GROUP):
            row = gi * NBR_ROW_GROUP + rr
            r = i * NBR_ROWS_PER_STEP + row
            r0 = jnp.clip(r - NA_ROWS // 2, 0, grid_rows - NA_ROWS)
            shift = r0 - r + NA_ROWS - 1
            wstart = pl.multiple_of(r0 * NA_COLS, NA_COLS)
            qoff = pl.multiple_of(row * GRID_W, GRID_W)
            per_tile = []
            for t in range(n_tiles):
                ts = slice(t * LANES, (t + 1) * LANES)
                qt = q_ref[0, pl.ds(qoff, GRID_W), ts].astype(F32)
                q_even = jnp.where(lower[GRID_W], qt, 0.0)
                q_odd = jnp.where(lower[GRID_W], 0.0, qt)
                qs = jnp.concatenate([part[lo:hi] for lo, hi, _ in blocks for part in (q_even, q_odd)], axis=0)
                s = _dot_nt(qs.astype(BF16), window(k_ref, wstart, ts))
                per_tile.append((s, window(v_ref, wstart, ts)))
            rows.append((shift, qoff, per_tile))
        for shift, qoff, per_tile in rows:
            tiles = []
            for t, (s, v_window) in enumerate(per_tile):
                seen = jnp.concatenate([s[2 * lo:2 * hi, w * LANES:(w + NBR_WINDOW_TILES) * LANES]
                                        for lo, hi, w in blocks], axis=0) + bias_ref[shift, t]
                p = jnp.exp(seen - jnp.max(seen, axis=-1, keepdims=True))
                inv_denom = 1.0 / jnp.sum(p, axis=-1, keepdims=True)
                p = p.astype(BF16)
                zeros = lambda rows_, n: [jnp.zeros((rows_, n * LANES), BF16)] if n else []
                p_full = jnp.concatenate(
                    [jnp.concatenate(zeros(2 * (hi - lo), w) + [p[2 * lo:2 * hi]]
                                     + zeros(2 * (hi - lo), NBR_COL_BLOCKS - NBR_WINDOW_TILES - w), axis=1)
                     for lo, hi, w in blocks], axis=0)
                o = _dot(p_full, v_window) * inv_denom
                tiles.append(jnp.concatenate(
                    [jnp.where(lower[hi - lo], o[2 * lo:lo + hi], o[lo + hi:2 * hi]) for lo, hi, _ in blocks],
                    axis=0))
            ob = jnp.concatenate(tiles, axis=1)
            o_ref[0, pl.ds(qoff, GRID_W), :] = _rms(ob, g_ref[...]).astype(BF16)
        return carry

    lax.fori_loop(0, NBR_ROWS_PER_STEP // NBR_ROW_GROUP, row_group, 0)


def _window_start_col(qc):
    return np.clip(qc - NA_COLS // 2, 0, GRID_W - NA_COLS)


def _nbr_blocks():
    blocks = []
    for lo in range(0, GRID_W, SUBLANES):
        c0 = _window_start_col(np.arange(lo, lo + SUBLANES))
        first, last = int(c0.min()) // NA_COLS, (int(c0.max()) + NA_COLS - 1) // NA_COLS
        assert last - first < NBR_WINDOW_TILES
        cb = min(first, NBR_COL_BLOCKS - NBR_WINDOW_TILES)
        if blocks and blocks[-1][2] == cb:
            blocks[-1] = (blocks[-1][0], lo + SUBLANES, cb)
        else:
            blocks.append((lo, lo + SUBLANES, cb))
    return blocks


def _nbr_bias_table(rpb):
    qc = np.arange(GRID_W)[:, None]
    kc = np.arange(GRID_W)[None, :]
    c0 = _window_start_col(qc)
    valid = (kc >= c0) & (kc < c0 + NA_COLS)
    col_idx = np.clip(kc - qc, -(NA_COLS - 1), NA_COLS - 1) + NA_COLS - 1
    toeplitz = jnp.where(valid[None, None], rpb.astype(F32)[:, :, col_idx], NEG)
    row_idx = np.arange(NA_ROWS)[:, None] + np.arange(NA_ROWS)[None, :]
    slabs = toeplitz[:, row_idx]
    slabs = slabs.reshape(N_HEADS_B, NA_ROWS, NA_ROWS, GRID_W, NBR_COL_BLOCKS, NA_COLS)
    slabs = slabs.transpose(1, 0, 3, 4, 2, 5).reshape(NA_ROWS, N_HEADS_B // 2, 2, GRID_W, NBR_COL_BLOCKS, LANES)
    parts = []
    for lo, hi, cb in _nbr_blocks():
        seen = slabs[:, :, :, lo:hi, cb:cb + NBR_WINDOW_TILES]
        parts.append(seen.reshape(NA_ROWS, N_HEADS_B // 2, 2 * (hi - lo), NBR_WINDOW_TILES * LANES))
    return jnp.concatenate(parts, axis=2)


def _attn_nbr(qb, kb, vb, bias, gain):
    b, seq, _ = qb.shape
    grid_rows = seq // GRID_W
    tq = NBR_ROWS_PER_STEP * GRID_W
    kv_spec = pl.BlockSpec((1,) + kb.shape[1:], lambda bi, i: (bi, 0, 0, 0))
    q_spec = pl.BlockSpec((1, tq, D_B), lambda bi, i: (bi, i, 0))
    return pl.pallas_call(
        functools.partial(_attn_nbr_kernel, grid_rows=grid_rows),
        grid=(b, seq // tq),
        in_specs=[q_spec, kv_spec, kv_spec, _const_spec(bias.shape), _const_spec((1, D_B))],
        out_specs=q_spec,
        out_shape=jax.ShapeDtypeStruct((b, seq, D_B), BF16),
        compiler_params=pltpu.CompilerParams(dimension_semantics=("parallel", "arbitrary"),
                                             vmem_limit_bytes=VMEM_LIMIT),
        name="attn_nbr",
    )(qb, kb, vb, bias, gain)


def _out_ffn_kernel(x1_ref, oa_ref, ob_ref, wout_ref, mixpost_ref, pre2_ref, wgu_ref, wd_ref, post2_ref, fin_ref,
                    y_ref):
    mixes = [_dot(jnp.concatenate([oa, ob], axis=1), wout_ref[...])
             for oa, ob in zip(_subtiles(oa_ref), _subtiles(ob_ref))]
    x2s = [x1 + _rms(mix, mixpost_ref[...]) for x1, mix in zip(_subtiles(x1_ref), mixes)]
    fs = _swiglu([_rms(x2, pre2_ref[...]).astype(BF16) for x2 in x2s], wgu_ref, wd_ref)
    x3s = [x2 + 0.5 * _rms(f, post2_ref[...]) for x2, f in zip(x2s, fs)]
    _store_subtiles(y_ref, [_rms(x3, fin_ref[...]) for x3 in x3s])


def _out_ffn(x1, oa, ob, wout, mixpost, pre2, wgu, wd, post2, fin):
    n_tok = x1.shape[0]
    tm = TOKEN_TILE
    tok = lambda w: pl.BlockSpec((tm, w), lambda i: (i, 0))
    vec = _const_spec((1, D_MODEL))
    return pl.pallas_call(
        _out_ffn_kernel,
        grid=(n_tok // tm,),
        in_specs=[tok(D_MODEL), tok(D_A), tok(D_B), _const_spec(wout.shape), vec, vec, _const_spec(wgu.shape),
                  _const_spec(wd.shape), vec, vec],
        out_specs=tok(D_MODEL),
        out_shape=jax.ShapeDtypeStruct((n_tok, D_MODEL), F32),
        compiler_params=pltpu.CompilerParams(dimension_semantics=("parallel",), vmem_limit_bytes=VMEM_LIMIT),
        name="out_ffn",
    )(x1, oa, ob, wout, mixpost, pre2, wgu, wd, post2, fin)


def _rope_tables(seq):
    half = HEAD_DIM // 2
    inv = ROPE_THETA ** (-jnp.arange(0, HEAD_DIM, 2, dtype=F32) / HEAD_DIM)
    ang = jnp.arange(seq, dtype=F32)[:, None] * inv[None, :]
    cos, sin = jnp.cos(ang), jnp.sin(ang)
    reps = LANES // HEAD_DIM
    return jnp.tile(jnp.concatenate([cos, cos], axis=1), (1, reps)), jnp.tile(
        jnp.concatenate([-sin, sin], axis=1), (1, reps))


def _layer(x, p):
    b, seq, _ = x.shape
    cos_t, sin_t = _rope_tables(seq)
    x1, qa, ka, va, qb, kb, vb = _ffn_in(x.reshape(b * seq, D_MODEL), seq, cos_t, sin_t, p["ffn1_pre"], p["ffn1_w_gu"],
                                         p["ffn1_w_down"], p["ffn1_post"], p["mix_pre"], p["w_in"])
    r3 = lambda a: a.reshape(b, seq, a.shape[-1])
    oa = _attn_win(r3(qa), r3(ka), r3(va), p["sink_a"], p["out_norm_a"])
    ob = _attn_nbr(r3(qb), kb, vb, p["nbr_bias"], p["out_norm_b"])
    y = _out_ffn(x1, oa.reshape(b * seq, D_A), ob.reshape(b * seq, D_B), p["w_out"], p["mix_post"], p["ffn2_pre"],
                 p["ffn2_w_gu"], p["ffn2_w_down"], p["ffn2_post"], p["final_norm"])
    return y.reshape(b, seq, D_MODEL)


def kernel(x_prompt, x_sample, ffn1_pre, ffn1_w_gu, ffn1_w_down, ffn1_post, mix_pre, w_in, sink_a, rpb_b, out_norm_a,
           out_norm_b, w_out, mix_post, ffn2_pre, ffn2_w_gu, ffn2_w_down, ffn2_post, final_norm):
    y_prompt, y_sample = x_prompt, x_sample
    half = N_HEADS_A // 2
    head_order = [h for t in range(half) for h in (t, t + half)]
    perm = np.concatenate([np.arange(h * HEAD_DIM, (h + 1) * HEAD_DIM) for h in head_order])
    in_cols = np.concatenate([perm, np.arange(D_A, D_IN)])
    out_rows = np.concatenate([perm, np.arange(D_A, D_A + D_B)])
    for l in range(ffn1_pre.shape[0]):
        row = lambda a: a[l].astype(F32).reshape(1, -1)
        p = {
            "ffn1_pre": row(ffn1_pre), "ffn1_post": row(ffn1_post), "mix_pre": row(mix_pre),
            "out_norm_a": row(out_norm_a)[:, perm], "out_norm_b": row(out_norm_b), "mix_post": row(mix_post),
            "ffn2_pre": row(ffn2_pre), "ffn2_post": row(ffn2_post), "final_norm": row(final_norm),
            "ffn1_w_gu": ffn1_w_gu[l].astype(BF16), "ffn1_w_down": ffn1_w_down[l].astype(BF16),
            "ffn2_w_gu": ffn2_w_gu[l].astype(BF16), "ffn2_w_down": ffn2_w_down[l].astype(BF16),
            "w_in": w_in[l][:, in_cols].astype(BF16), "w_out": w_out[l][out_rows, :].astype(BF16),
            "sink_a": sink_a[l].astype(F32), "nbr_bias": _nbr_bias_table(rpb_b[l]),
        }
        y_prompt = _layer(y_prompt, p)
        y_sample = _layer(y_sample, p)
    return (y_prompt, y_sample)
```

```python
import functools

import jax
import jax.numpy as jnp
import numpy as np
from jax import lax
from jax.experimental import pallas as pl
from jax.experimental.pallas import tpu as pltpu

D_MODEL = 1024
HEAD_DIM = 64
N_HEADS_A = 8
N_KV_A = 2
N_HEADS_B = 8
WIN = 128
BLK = 128
GRID_W = 64
NA_ROWS = 8
NA_COLS = 16
D_FF = 2816
ROPE_THETA = 10000.0
EPS = 1e-6
D_A = N_HEADS_A * HEAD_DIM
D_KV_A = N_KV_A * HEAD_DIM
D_B = N_HEADS_B * HEAD_DIM
D_IN = D_A + 2 * D_KV_A + 3 * D_B
D_ROPE = D_A + D_KV_A
NEG = -1e30
SCALE = HEAD_DIM ** -0.5
LOG2E = 1.4426950408889634

LANES = 128
MXU_DIM = 256
FFN_CHUNKS = ((0, 6 * MXU_DIM), (6 * MXU_DIM, 5 * MXU_DIM))
TOKEN_TILE = 512
TOKEN_SUBTILE_ROWS = (192, 192, 128)
WIN_Q_TILE = 1024
NBR_ROWS_PER_STEP = 8
NBR_ROW_GROUP = 8
NBR_KEYS = NA_ROWS * GRID_W
NBR_COL_BLOCKS = GRID_W // NA_COLS
NBR_WINDOW_TILES = 2
SUBLANES = 8
VMEM_LIMIT = 56 * 1024 * 1024

BF16 = jnp.bfloat16
F32 = jnp.float32

assert sum(w for _, w in FFN_CHUNKS) == D_FF
assert NA_ROWS * NA_COLS == LANES


def _dot(a, b):
    return jnp.dot(a, b, preferred_element_type=F32)


def _dot_nt(a, b):
    return lax.dot_general(a, b, (((1,), (1,)), ((), ())), preferred_element_type=F32)


def _rms(x, g):
    return x * lax.rsqrt(jnp.mean(x * x, axis=-1, keepdims=True) + EPS) * g


def _subtile_bounds():
    edges = np.cumsum((0,) + TOKEN_SUBTILE_ROWS)
    return list(zip(edges[:-1], edges[1:]))


def _subtiles(ref):
    return [ref[lo:hi, :] for lo, hi in _subtile_bounds()]


def _store_subtiles(ref, values):
    for (lo, hi), v in zip(_subtile_bounds(), values):
        ref[lo:hi, :] = v.astype(ref.dtype)


def _swiglu(xns, wgu_ref, wd_ref):
    accs = [None] * len(xns)
    for lo, width in FFN_CHUNKS:
        hs = []
        for xn in xns:
            g = _dot(xn, wgu_ref[:, lo:lo + width])
            u = _dot(xn, wgu_ref[:, D_FF + lo:D_FF + lo + width])
            hs.append((g * jax.nn.sigmoid(g) * u).astype(BF16))
        for k, h in enumerate(hs):
            part = _dot(h, wd_ref[lo:lo + width, :])
            accs[k] = part if accs[k] is None else accs[k] + part
    return accs


def _rope_qkv(proj, cos, sin):
    lane = lax.broadcasted_iota(jnp.int32, cos.shape, 1)
    first_half = (lane % HEAD_DIM) < (HEAD_DIM // 2)
    roped = []
    for t in range(D_ROPE // LANES):
        xt = proj[:, t * LANES:(t + 1) * LANES]
        partner = jnp.where(first_half,
                            pltpu.roll(xt, LANES - HEAD_DIM // 2, 1),
                            pltpu.roll(xt, HEAD_DIM // 2, 1))
        roped.append(xt * cos + partner * sin)
    o = D_ROPE
    va = proj[:, o:o + D_KV_A]
    o += D_KV_A
    qb = proj[:, o:o + D_B] * SCALE
    o += D_B
    kb = proj[:, o:o + D_B]
    o += D_B
    vb = proj[:, o:o + D_B]
    return jnp.concatenate(roped[:D_A // LANES], axis=1) * (SCALE * LOG2E), roped[D_A // LANES], va, qb, kb, vb


def _ffn_in_kernel(x_ref, cos_ref, sin_ref, pre1_ref, wgu_ref, wd_ref, post1_ref, mixpre_ref, win_ref,
                   x1_ref, qa_ref, ka_ref, va_ref, qb_ref, kb_ref, vb_ref):
    xs = _subtiles(x_ref)
    fs = _swiglu([_rms(x, pre1_ref[...]).astype(BF16) for x in xs], wgu_ref, wd_ref)
    x1s = [x + 0.5 * _rms(f, post1_ref[...]) for x, f in zip(xs, fs)]
    _store_subtiles(x1_ref, x1s)
    projs = [_dot(_rms(x1, mixpre_ref[...]).astype(BF16), win_ref[...]) for x1 in x1s]

    outs = [_rope_qkv(proj, cos, sin) for proj, cos, sin in zip(projs, _subtiles(cos_ref), _subtiles(sin_ref))]
    qa, ka, va, qb, kb, vb = zip(*outs)
    for ref, values in zip((qa_ref, ka_ref, va_ref, qb_ref), (qa, ka, va, qb)):
        _store_subtiles(ref, values)
    _store_col_blocked(kb_ref, kb)
    _store_col_blocked(vb_ref, vb)


def _store_col_blocked(ref, values):
    grid_row = 0
    for v in values:
        v = v.astype(ref.dtype)
        for r in range(v.shape[0] // GRID_W):
            dst = grid_row * NA_COLS
            for cb in range(NBR_COL_BLOCKS):
                src = r * GRID_W + cb * NA_COLS
                ref[0, cb, dst:dst + NA_COLS, :] = v[src:src + NA_COLS, :]
            grid_row += 1


def _const_spec(shape):
    return pl.BlockSpec(shape, lambda *_: (0,) * len(shape), pipeline_mode=pl.Buffered(1))


def _ffn_in(x2d, seq, cos_t, sin_t, pre1, wgu, wd, post1, mixpre, win):
    n_tok = x2d.shape[0]
    tm = TOKEN_TILE
    tiles_per_seq = seq // tm
    tok = lambda w: pl.BlockSpec((tm, w), lambda i: (i, 0))
    rope_spec = pl.BlockSpec((tm, LANES), lambda i: (i % tiles_per_seq, 0))
    vec = _const_spec((1, D_MODEL))
    out_widths = (D_A, D_KV_A, D_KV_A, D_B)
    blocked_rows = tm // NBR_COL_BLOCKS
    blocked_spec = pl.BlockSpec((1, NBR_COL_BLOCKS, blocked_rows, D_B),
                                lambda i: (i // tiles_per_seq, 0, i % tiles_per_seq, 0))
    blocked_shape = jax.ShapeDtypeStruct((n_tok // seq, NBR_COL_BLOCKS, seq // NBR_COL_BLOCKS, D_B), BF16)
    return pl.pallas_call(
        _ffn_in_kernel,
        grid=(n_tok // tm,),
        in_specs=[tok(D_MODEL), rope_spec, rope_spec, vec, _const_spec(wgu.shape), _const_spec(wd.shape), vec, vec,
                  _const_spec(win.shape)],
        out_specs=[tok(D_MODEL)] + [tok(w) for w in out_widths] + [blocked_spec, blocked_spec],
        out_shape=[jax.ShapeDtypeStruct((n_tok, D_MODEL), F32)]
        + [jax.ShapeDtypeStruct((n_tok, w), BF16) for w in out_widths] + [blocked_shape, blocked_shape],
        compiler_params=pltpu.CompilerParams(dimension_semantics=("parallel",), vmem_limit_bytes=VMEM_LIMIT),
        name="ffn_in",
    )(x2d, cos_t, sin_t, pre1, wgu, wd, post1, mixpre, win)


def _attn_win_kernel(sink_ref, q_ref, k_ref, v_ref, g_ref, o_ref, *, seq):
    i = pl.program_id(1)
    nk = BLK + 2 * WIN
    n_tiles = D_A // LANES
    n_blocks = WIN_Q_TILE // BLK
    lower = lax.broadcasted_iota(jnp.int32, (BLK, LANES), 1) < HEAD_DIM
    rel = lax.broadcasted_iota(jnp.int32, (BLK, nk), 1) - lax.broadcasted_iota(jnp.int32, (BLK, nk), 0)
    block_of_row = lax.broadcasted_iota(jnp.int32, (2 * n_tiles * BLK, 1), 0) // BLK
    sink = jnp.zeros((2 * n_tiles * BLK, 1), F32)
    for t in range(n_tiles):
        for u in range(2):
            sink = jnp.where(block_of_row == 2 * t + u, sink_ref[t + u * n_tiles] * LOG2E, sink)

    def window_start(j):
        q0 = i * WIN_Q_TILE + j * BLK
        start = pl.multiple_of(jnp.clip(q0 - WIN, 0, seq - nk), BLK)
        return q0, start

    scores = []
    for j in range(n_blocks):
        _, start = window_start(j)
        stacked = []
        for t in range(n_tiles):
            qt = q_ref[0, j * BLK:(j + 1) * BLK, t * LANES:(t + 1) * LANES]
            stacked += [jnp.where(lower, qt, jnp.zeros_like(qt)), jnp.where(lower, jnp.zeros_like(qt), qt)]
        scores.append(_dot_nt(jnp.concatenate(stacked, axis=0), k_ref[0, pl.ds(start, nk), :]))

    for j in range(n_blocks):
        q0, start = window_start(j)
        mask = jnp.where(jnp.abs(rel - (q0 - start)) <= WIN, 0.0, NEG)
        s = scores[j] + jnp.concatenate([mask] * (2 * n_tiles), axis=0)
        m = jnp.maximum(jnp.max(s, axis=-1, keepdims=True), sink)
        p = jnp.exp2(s - m)
        denom = jnp.sum(p, axis=-1, keepdims=True) + jnp.exp2(sink - m)
        o = _dot(p.astype(BF16), v_ref[0, pl.ds(start, nk), :]) * (1.0 / denom)
        tiles = [jnp.where(lower, o[2 * t * BLK:(2 * t + 1) * BLK], o[(2 * t + 1) * BLK:(2 * t + 2) * BLK])
                 for t in range(n_tiles)]
        oa = jnp.concatenate(tiles, axis=1)
        o_ref[0, j * BLK:(j + 1) * BLK, :] = _rms(oa, g_ref[...]).astype(BF16)


def _attn_win(qa, ka, va, sink, gain):
    b, seq, _ = qa.shape
    kv_spec = pl.BlockSpec((1, seq, D_KV_A), lambda bi, i: (bi, 0, 0))
    q_spec = pl.BlockSpec((1, WIN_Q_TILE, D_A), lambda bi, i: (bi, i, 0))
    return pl.pallas_call(
        functools.partial(_attn_win_kernel, seq=seq),
        grid=(b, seq // WIN_Q_TILE),
        in_specs=[pl.BlockSpec(memory_space=pltpu.SMEM), q_spec, kv_spec, kv_spec, _const_spec((1, D_A))],
        out_specs=q_spec,
        out_shape=jax.ShapeDtypeStruct((b, seq, D_A), BF16),
        compiler_params=pltpu.CompilerParams(dimension_semantics=("parallel", "arbitrary"),
                                             vmem_limit_bytes=VMEM_LIMIT),
        name="attn_win",
    )(sink, qa, ka, va, gain)


def _attn_nbr_kernel(q_ref, k_ref, v_ref, bias_ref, g_ref, o_ref, *, grid_rows):
    i = pl.program_id(1)
    n_tiles = D_B // LANES
    blocks = _nbr_blocks()
    lower = {n: lax.broadcasted_iota(jnp.int32, (n, LANES), 1) < HEAD_DIM
             for n in {GRID_W} | {hi - lo for lo, hi, _ in blocks}}

    def window(ref, start, ts):
        return jnp.concatenate([ref[0, cb, pl.ds(start, LANES), ts] for cb in range(NBR_COL_BLOCKS)], axis=0)

    def row_group(gi, carry):
        rows = []
        for rr in range(NBR_ROW_GROUP):
            row = gi * NBR_ROW_GROUP + rr
            r = i * NBR_ROWS_PER_STEP + row
            r0 = jnp.clip(r - NA_ROWS // 2, 0, grid_rows - NA_ROWS)
            shift = r0 - r + NA_ROWS - 1
            wstart = pl.multiple_of(r0 * NA_COLS, NA_COLS)
            qoff = pl.multiple_of(row * GRID_W, GRID_W)
            per_tile = []
            for t in range(n_tiles):
                ts = slice(t * LANES, (t + 1) * LANES)
                qt = q_ref[0, pl.ds(qoff, GRID_W), ts].astype(F32)
                q_even = jnp.where(lower[GRID_W], qt, 0.0)
                q_odd = jnp.where(lower[GRID_W], 0.0, qt)
                qs = jnp.concatenate([part[lo:hi] for lo, hi, _ in blocks for part in (q_even, q_odd)], axis=0)
                s = _dot_nt(qs.astype(BF16), window(k_ref, wstart, ts))
                per_tile.append((s, window(v_ref, wstart, ts)))
            rows.append((shift, qoff, per_tile))
        for shift, qoff, per_tile in rows:
            tiles = []
            for t, (s, v_window) in enumerate(per_tile):
                seen = jnp.concatenate([s[2 * lo:2 * hi, w * LANES:(w + NBR_WINDOW_TILES) * LANES]
                                        for lo, hi, w in blocks], axis=0) + bias_ref[shift, t]
                p = jnp.exp(seen - jnp.max(seen, axis=-1, keepdims=True))
                inv_denom = 1.0 / jnp.sum(p, axis=-1, keepdims=True)
                p = p.astype(BF16)
                zeros = lambda rows_, n: [jnp.zeros((rows_, n * LANES), BF16)] if n else []
                p_full = jnp.concatenate(
                    [jnp.concatenate(zeros(2 * (hi - lo), w) + [p[2 * lo:2 * hi]]
                                     + zeros(2 * (hi - lo), NBR_COL_BLOCKS - NBR_WINDOW_TILES - w), axis=1)
                     for lo, hi, w in blocks], axis=0)
                o = _dot(p_full, v_window) * inv_denom
                tiles.append(jnp.concatenate(
                    [jnp.where(lower[hi - lo], o[2 * lo:lo + hi], o[lo + hi:2 * hi]) for lo, hi, _ in blocks],
                    axis=0))
            ob = jnp.concatenate(tiles, axis=1)
            o_ref[0, pl.ds(qoff, GRID_W), :] = _rms(ob, g_ref[...]).astype(BF16)
        return carry

    lax.fori_loop(0, NBR_ROWS_PER_STEP // NBR_ROW_GROUP, row_group, 0)


def _window_start_col(qc):
    return np.clip(qc - NA_COLS // 2, 0, GRID_W - NA_COLS)


def _nbr_blocks():
    blocks = []
    for lo in range(0, GRID_W, SUBLANES):
        c0 = _window_start_col(np.arange(lo, lo + SUBLANES))
        first, last = int(c0.min()) // NA_COLS, (int(c0.max()) + NA_COLS - 1) // NA_COLS
        assert last - first < NBR_WINDOW_TILES
        cb = min(first, NBR_COL_BLOCKS - NBR_WINDOW_TILES)
        if blocks and blocks[-1][2] == cb:
            blocks[-1] = (blocks[-1][0], lo + SUBLANES, cb)
        else:
            blocks.append((lo, lo + SUBLANES, cb))
    return blocks


def _nbr_bias_table(rpb):
    qc = np.arange(GRID_W)[:, None]
    kc = np.arange(GRID_W)[None, :]
    c0 = _window_start_col(qc)
    valid = (kc >= c0) & (kc < c0 + NA_COLS)
    period = 2 * GRID_W - 1
    pad = GRID_W - NA_COLS
    v = jnp.pad(rpb.astype(F32), ((0, 0), (0, 0), (pad, pad)))
    assert v.shape[-1] == period
    skew = jnp.tile(v, (1, 1, GRID_W + 1))[..., :GRID_W * (period + 1)]
    skew = skew.reshape(v.shape[:2] + (GRID_W, period + 1))
    toeplitz = skew[:, :, ::-1, :GRID_W]
    toeplitz = jnp.where(valid[None, None], toeplitz, NEG)
    slabs = jnp.stack([toeplitz[:, t:t + NA_ROWS] for t in range(NA_ROWS)], axis=1)
    slabs = slabs.reshape(N_HEADS_B, NA_ROWS, NA_ROWS, GRID_W, NBR_COL_BLOCKS, NA_COLS)
    slabs = slabs.transpose(1, 0, 3, 4, 2, 5).reshape(NA_ROWS, N_HEADS_B // 2, 2, GRID_W, NBR_COL_BLOCKS, LANES)
    parts = []
    for lo, hi, cb in _nbr_blocks():
        seen = slabs[:, :, :, lo:hi, cb:cb + NBR_WINDOW_TILES]
        parts.append(seen.reshape(NA_ROWS, N_HEADS_B // 2, 2 * (hi - lo), NBR_WINDOW_TILES * LANES))
    return jnp.concatenate(parts, axis=2)


def _attn_nbr(qb, kb, vb, bias, gain):
    b, seq, _ = qb.shape
    grid_rows = seq // GRID_W
    tq = NBR_ROWS_PER_STEP * GRID_W
    kv_spec = pl.BlockSpec((1,) + kb.shape[1:], lambda bi, i: (bi, 0, 0, 0))
    q_spec = pl.BlockSpec((1, tq, D_B), lambda bi, i: (bi, i, 0))
    return pl.pallas_call(
        functools.partial(_attn_nbr_kernel, grid_rows=grid_rows),
        grid=(b, seq // tq),
        in_specs=[q_spec, kv_spec, kv_spec, _const_spec(bias.shape), _const_spec((1, D_B))],
        out_specs=q_spec,
        out_shape=jax.ShapeDtypeStruct((b, seq, D_B), BF16),
        compiler_params=pltpu.CompilerParams(dimension_semantics=("parallel", "arbitrary"),
                                             vmem_limit_bytes=VMEM_LIMIT),
        name="attn_nbr",
    )(qb, kb, vb, bias, gain)


def _out_ffn_kernel(x1_ref, oa_ref, ob_ref, wout_ref, mixpost_ref, pre2_ref, wgu_ref, wd_ref, post2_ref, fin_ref,
                    y_ref):
    mixes = [_dot(jnp.concatenate([oa, ob], axis=1), wout_ref[...])
             for oa, ob in zip(_subtiles(oa_ref), _subtiles(ob_ref))]
    x2s = [x1 + _rms(mix, mixpost_ref[...]) for x1, mix in zip(_subtiles(x1_ref), mixes)]
    fs = _swiglu([_rms(x2, pre2_ref[...]).astype(BF16) for x2 in x2s], wgu_ref, wd_ref)
    x3s = [x2 + 0.5 * _rms(f, post2_ref[...]) for x2, f in zip(x2s, fs)]
    _store_subtiles(y_ref, [_rms(x3, fin_ref[...]) for x3 in x3s])


def _out_ffn(x1, oa, ob, wout, mixpost, pre2, wgu, wd, post2, fin):
    n_tok = x1.shape[0]
    tm = TOKEN_TILE
    tok = lambda w: pl.BlockSpec((tm, w), lambda i: (i, 0))
    vec = _const_spec((1, D_MODEL))
    return pl.pallas_call(
        _out_ffn_kernel,
        grid=(n_tok // tm,),
        in_specs=[tok(D_MODEL), tok(D_A), tok(D_B), _const_spec(wout.shape), vec, vec, _const_spec(wgu.shape),
                  _const_spec(wd.shape), vec, vec],
        out_specs=tok(D_MODEL),
        out_shape=jax.ShapeDtypeStruct((n_tok, D_MODEL), F32),
        compiler_params=pltpu.CompilerParams(dimension_semantics=("parallel",), vmem_limit_bytes=VMEM_LIMIT),
        name="out_ffn",
    )(x1, oa, ob, wout, mixpost, pre2, wgu, wd, post2, fin)


def _rope_tables(seq):
    half = HEAD_DIM // 2
    inv = ROPE_THETA ** (-jnp.arange(0, HEAD_DIM, 2, dtype=F32) / HEAD_DIM)
    reps = 2 * LANES // HEAD_DIM
    ang = jnp.arange(seq, dtype=F32)[:, None] * jnp.tile(inv, reps)[None, :]
    sign = np.tile(np.repeat(np.array([-1.0, 1.0], np.float32), half), LANES // HEAD_DIM)
    return jnp.cos(ang), jnp.sin(ang) * sign[None, :]


def _layer(x, p):
    b, seq, _ = x.shape
    cos_t, sin_t = _rope_tables(seq)
    x1, qa, ka, va, qb, kb, vb = _ffn_in(x.reshape(b * seq, D_MODEL), seq, cos_t, sin_t, p["ffn1_pre"], p["ffn1_w_gu"],
                                         p["ffn1_w_down"], p["ffn1_post"], p["mix_pre"], p["w_in"])
    r3 = lambda a: a.reshape(b, seq, a.shape[-1])
    oa = _attn_win(r3(qa), r3(ka), r3(va), p["sink_a"], p["out_norm_a"])
    ob = _attn_nbr(r3(qb), kb, vb, p["nbr_bias"], p["out_norm_b"])
    y = _out_ffn(x1, oa.reshape(b * seq, D_A), ob.reshape(b * seq, D_B), p["w_out"], p["mix_post"], p["ffn2_pre"],
                 p["ffn2_w_gu"], p["ffn2_w_down"], p["ffn2_post"], p["final_norm"])
    return y.reshape(b, seq, D_MODEL)


def kernel(x_prompt, x_sample, ffn1_pre, ffn1_w_gu, ffn1_w_down, ffn1_post, mix_pre, w_in, sink_a, rpb_b, out_norm_a,
           out_norm_b, w_out, mix_post, ffn2_pre, ffn2_w_gu, ffn2_w_down, ffn2_post, final_norm):
    y_prompt, y_sample = x_prompt, x_sample
    group = N_HEADS_A // N_KV_A

    def regroup(a, axis):
        shape = a.shape
        a = a.reshape(shape[:axis] + (N_KV_A, group, HEAD_DIM) + shape[axis + 1:])
        return jnp.swapaxes(a, axis, axis + 1).reshape(shape)

    for l in range(ffn1_pre.shape[0]):
        row = lambda a: a[l].astype(F32).reshape(1, -1)
        w_in_l = jnp.concatenate([regroup(w_in[l][:, :D_A], 1), w_in[l][:, D_A:]], axis=1)
        w_out_l = jnp.concatenate([regroup(w_out[l][:D_A], 0), w_out[l][D_A:]], axis=0)
        p = {
            "ffn1_pre": row(ffn1_pre), "ffn1_post": row(ffn1_post), "mix_pre": row(mix_pre),
            "out_norm_a": regroup(row(out_norm_a), 1), "out_norm_b": row(out_norm_b), "mix_post": row(mix_post),
            "ffn2_pre": row(ffn2_pre), "ffn2_post": row(ffn2_post), "final_norm": row(final_norm),
            "ffn1_w_gu": ffn1_w_gu[l].astype(BF16), "ffn1_w_down": ffn1_w_down[l].astype(BF16),
            "ffn2_w_gu": ffn2_w_gu[l].astype(BF16), "ffn2_w_down": ffn2_w_down[l].astype(BF16),
            "w_in": w_in_l.astype(BF16), "w_out": w_out_l.astype(BF16),
            "sink_a": sink_a[l].astype(F32), "nbr_bias": _nbr_bias_table(rpb_b[l]),
        }
        y_prompt = _layer(y_prompt, p)
        y_sample = _layer(y_sample, p)
    return (y_prompt, y_sample)
```

```python
import functools

import jax
import jax.numpy as jnp
import numpy as np
from jax import lax
from jax.experimental import pallas as pl
from jax.experimental.pallas import tpu as pltpu

D_MODEL = 1024
HEAD_DIM = 64
N_HEADS_A = 8
N_KV_A = 2
N_HEADS_B = 8
WIN = 128
BLK = 128
GRID_W = 64
NA_ROWS = 8
NA_COLS = 16
D_FF = 2816
ROPE_THETA = 10000.0
EPS = 1e-6
D_A = N_HEADS_A * HEAD_DIM
D_KV_A = N_KV_A * HEAD_DIM
D_B = N_HEADS_B * HEAD_DIM
D_IN = D_A + 2 * D_KV_A + 3 * D_B
D_ROPE = D_A + D_KV_A
NEG = -1e30
SCALE = HEAD_DIM ** -0.5
LOG2E = 1.4426950408889634

LANES = 128
MXU_DIM = 256
FFN_CHUNKS = ((0, 6 * MXU_DIM), (6 * MXU_DIM, 5 * MXU_DIM))
TOKEN_TILE = 512
OUT_FFN_PASSES = 2
IN_FFN_PASSES = 2
TOKEN_SUBTILE_ROWS = (192, 192, 128)
WIN_Q_TILE = 1024
NBR_ROWS_PER_STEP = 8
NBR_ROW_GROUP = 8
NBR_KEYS = NA_ROWS * GRID_W
NBR_COL_BLOCKS = GRID_W // NA_COLS
NBR_WINDOW_TILES = 2
SUBLANES = 8
VMEM_BYTES = 64 * 1024 * 1024
VMEM_LIMIT = VMEM_BYTES - 4 * 1024 * 1024

BF16 = jnp.bfloat16
F32 = jnp.float32

assert sum(w for _, w in FFN_CHUNKS) == D_FF
assert NA_ROWS * NA_COLS == LANES


def _dot(a, b):
    return jnp.dot(a, b, preferred_element_type=F32)


def _dot_nt(a, b):
    return lax.dot_general(a, b, (((1,), (1,)), ((), ())), preferred_element_type=F32)


def _rms(x, g):
    return x * lax.rsqrt(jnp.mean(x * x, axis=-1, keepdims=True) + EPS) * g


def _subtile_bounds(base=0):
    edges = base + np.cumsum((0,) + TOKEN_SUBTILE_ROWS)
    return list(zip(edges[:-1], edges[1:]))


def _subtiles(ref, base=0):
    return [ref[lo:hi, :] for lo, hi in _subtile_bounds(base)]


def _store_subtiles(ref, values, base=0):
    for (lo, hi), v in zip(_subtile_bounds(base), values):
        ref[lo:hi, :] = v.astype(ref.dtype)


def _swiglu(xns, wgu_ref, wd_ref):
    accs = [None] * len(xns)
    for lo, width in FFN_CHUNKS:
        hs = []
        for xn in xns:
            g = _dot(xn, wgu_ref[:, lo:lo + width])
            u = _dot(xn, wgu_ref[:, D_FF + lo:D_FF + lo + width])
            hs.append((g * jax.nn.sigmoid(g) * u).astype(BF16))
        for k, h in enumerate(hs):
            part = _dot(h, wd_ref[lo:lo + width, :])
            accs[k] = part if accs[k] is None else accs[k] + part
    return accs


def _rope_qkv(proj, cos, sin):
    lane = lax.broadcasted_iota(jnp.int32, cos.shape, 1)
    first_half = (lane % HEAD_DIM) < (HEAD_DIM // 2)
    roped = []
    for t in range(D_ROPE // LANES):
        xt = proj[:, t * LANES:(t + 1) * LANES]
        partner = jnp.where(first_half,
                            pltpu.roll(xt, LANES - HEAD_DIM // 2, 1),
                            pltpu.roll(xt, HEAD_DIM // 2, 1))
        roped.append(xt * cos + partner * sin)
    o = D_ROPE
    va = proj[:, o:o + D_KV_A]
    o += D_KV_A
    qb = proj[:, o:o + D_B] * SCALE
    o += D_B
    kb = proj[:, o:o + D_B]
    o += D_B
    vb = proj[:, o:o + D_B]
    return jnp.concatenate(roped[:D_A // LANES], axis=1) * (SCALE * LOG2E), roped[D_A // LANES], va, qb, kb, vb


def _ffn_in_kernel(x_ref, cos_ref, sin_ref, pre1_ref, wgu_ref, wd_ref, post1_ref, mixpre_ref, win_ref,
                   x1_ref, qa_ref, ka_ref, va_ref, qb_ref, kb_ref, vb_ref):
    for base in range(0, x_ref.shape[0], TOKEN_TILE):
        xs = _subtiles(x_ref, base)
        fs = _swiglu([_rms(x, pre1_ref[...]).astype(BF16) for x in xs], wgu_ref, wd_ref)
        x1s = [x + 0.5 * _rms(f, post1_ref[...]) for x, f in zip(xs, fs)]
        _store_subtiles(x1_ref, x1s, base)
        projs = [_dot(_rms(x1, mixpre_ref[...]).astype(BF16), win_ref[...]) for x1 in x1s]

        outs = [_rope_qkv(proj, cos, sin)
                for proj, cos, sin in zip(projs, _subtiles(cos_ref, base), _subtiles(sin_ref, base))]
        qa, ka, va, qb, kb, vb = zip(*outs)
        for ref, values in zip((qa_ref, ka_ref, va_ref, qb_ref), (qa, ka, va, qb)):
            _store_subtiles(ref, values, base)
        _store_col_blocked(kb_ref, kb, base // GRID_W)
        _store_col_blocked(vb_ref, vb, base // GRID_W)


def _store_col_blocked(ref, values, grid_row):
    for v in values:
        v = v.astype(ref.dtype)
        for r in range(v.shape[0] // GRID_W):
            dst = grid_row * NA_COLS
            for cb in range(NBR_COL_BLOCKS):
                src = r * GRID_W + cb * NA_COLS
                ref[0, cb, dst:dst + NA_COLS, :] = v[src:src + NA_COLS, :]
            grid_row += 1


def _const_spec(shape):
    return pl.BlockSpec(shape, lambda *_: (0,) * len(shape), pipeline_mode=pl.Buffered(1))


def _ffn_in(x2d, seq, cos_t, sin_t, pre1, wgu, wd, post1, mixpre, win):
    n_tok = x2d.shape[0]
    tm = IN_FFN_PASSES * TOKEN_TILE
    tiles_per_seq = seq // tm
    tok = lambda w: pl.BlockSpec((tm, w), lambda i: (i, 0))
    rope_spec = pl.BlockSpec((tm, LANES), lambda i: (i % tiles_per_seq, 0))
    vec = _const_spec((1, D_MODEL))
    out_widths = (D_A, D_KV_A, D_KV_A, D_B)
    blocked_rows = tm // NBR_COL_BLOCKS
    blocked_spec = pl.BlockSpec((1, NBR_COL_BLOCKS, blocked_rows, D_B),
                                lambda i: (i // tiles_per_seq, 0, i % tiles_per_seq, 0))
    blocked_shape = jax.ShapeDtypeStruct((n_tok // seq, NBR_COL_BLOCKS, seq // NBR_COL_BLOCKS, D_B), BF16)
    return pl.pallas_call(
        _ffn_in_kernel,
        grid=(n_tok // tm,),
        in_specs=[tok(D_MODEL), rope_spec, rope_spec, vec, _const_spec(wgu.shape), _const_spec(wd.shape), vec, vec,
                  _const_spec(win.shape)],
        out_specs=[tok(D_MODEL)] + [tok(w) for w in out_widths] + [blocked_spec, blocked_spec],
        out_shape=[jax.ShapeDtypeStruct((n_tok, D_MODEL), F32)]
        + [jax.ShapeDtypeStruct((n_tok, w), BF16) for w in out_widths] + [blocked_shape, blocked_shape],
        compiler_params=pltpu.CompilerParams(dimension_semantics=("parallel",), vmem_limit_bytes=VMEM_LIMIT),
        name="ffn_in",
    )(x2d, cos_t, sin_t, pre1, wgu, wd, post1, mixpre, win)


def _attn_win_kernel(sink_ref, q_ref, k_ref, v_ref, g_ref, o_ref, *, seq):
    i = pl.program_id(1)
    nk = BLK + 2 * WIN
    n_tiles = D_A // LANES
    n_blocks = WIN_Q_TILE // BLK
    lower = lax.broadcasted_iota(jnp.int32, (BLK, LANES), 1) < HEAD_DIM
    rel = lax.broadcasted_iota(jnp.int32, (BLK, nk), 1) - lax.broadcasted_iota(jnp.int32, (BLK, nk), 0)
    block_of_row = lax.broadcasted_iota(jnp.int32, (2 * n_tiles * BLK, 1), 0) // BLK
    sink = jnp.zeros((2 * n_tiles * BLK, 1), F32)
    for t in range(n_tiles):
        for u in range(2):
            sink = jnp.where(block_of_row == 2 * t + u, sink_ref[t + u * n_tiles] * LOG2E, sink)

    def window_start(j):
        q0 = i * WIN_Q_TILE + j * BLK
        start = pl.multiple_of(jnp.clip(q0 - WIN, 0, seq - nk), BLK)
        return q0, start

    scores = []
    for j in range(n_blocks):
        _, start = window_start(j)
        stacked = []
        for t in range(n_tiles):
            qt = q_ref[0, j * BLK:(j + 1) * BLK, t * LANES:(t + 1) * LANES]
            stacked += [jnp.where(lower, qt, jnp.zeros_like(qt)), jnp.where(lower, jnp.zeros_like(qt), qt)]
        scores.append(_dot_nt(jnp.concatenate(stacked, axis=0), k_ref[0, pl.ds(start, nk), :]))

    for j in range(n_blocks):
        q0, start = window_start(j)
        mask = jnp.where(jnp.abs(rel - (q0 - start)) <= WIN, 0.0, NEG)
        s = scores[j] + jnp.concatenate([mask] * (2 * n_tiles), axis=0)
        m = jnp.maximum(jnp.max(s, axis=-1, keepdims=True), sink)
        p = jnp.exp2(s - m)
        denom = jnp.sum(p, axis=-1, keepdims=True) + jnp.exp2(sink - m)
        o = _dot(p.astype(BF16), v_ref[0, pl.ds(start, nk), :]) * (1.0 / denom)
        tiles = [jnp.where(lower, o[2 * t * BLK:(2 * t + 1) * BLK], o[(2 * t + 1) * BLK:(2 * t + 2) * BLK])
                 for t in range(n_tiles)]
        oa = jnp.concatenate(tiles, axis=1)
        o_ref[0, j * BLK:(j + 1) * BLK, :] = _rms(oa, g_ref[...]).astype(BF16)


def _attn_win(qa, ka, va, sink, gain):
    b, seq, _ = qa.shape
    kv_spec = pl.BlockSpec((1, seq, D_KV_A), lambda bi, i: (bi, 0, 0))
    q_spec = pl.BlockSpec((1, WIN_Q_TILE, D_A), lambda bi, i: (bi, i, 0))
    return pl.pallas_call(
        functools.partial(_attn_win_kernel, seq=seq),
        grid=(b, seq // WIN_Q_TILE),
        in_specs=[pl.BlockSpec(memory_space=pltpu.SMEM), q_spec, kv_spec, kv_spec, _const_spec((1, D_A))],
        out_specs=q_spec,
        out_shape=jax.ShapeDtypeStruct((b, seq, D_A), BF16),
        compiler_params=pltpu.CompilerParams(dimension_semantics=("parallel", "arbitrary"),
                                             vmem_limit_bytes=VMEM_LIMIT),
        name="attn_win",
    )(sink, qa, ka, va, gain)


def _attn_nbr_kernel(q_ref, k_ref, v_ref, bias_ref, g_ref, o_ref, *, grid_rows):
    i = pl.program_id(1)
    n_tiles = D_B // LANES
    blocks = _nbr_blocks()
    lower = {n: lax.broadcasted_iota(jnp.int32, (n, LANES), 1) < HEAD_DIM
             for n in {GRID_W} | {hi - lo for lo, hi, _ in blocks}}

    key_groups = _nbr_key_groups(blocks)

    def window(ref, start, ts, tile_lo, tile_hi):
        return jnp.concatenate([ref[0, cb, pl.ds(start, LANES), ts] for cb in range(tile_lo, tile_hi)], axis=0)

    def group_piece(per_group, lo, hi, tile):
        for (t_lo, t_hi, r_lo, _), x in zip(key_groups, per_group):
            if t_lo <= tile < t_hi:
                return x[2 * lo - r_lo:2 * hi - r_lo, (tile - t_lo) * LANES:(tile - t_lo + 1) * LANES]

    def row_group(gi, carry):
        rows = []
        for rr in range(NBR_ROW_GROUP):
            row = gi * NBR_ROW_GROUP + rr
            r = i * NBR_ROWS_PER_STEP + row
            r0 = jnp.clip(r - NA_ROWS // 2, 0, grid_rows - NA_ROWS)
            shift = r0 - r + NA_ROWS - 1
            wstart = pl.multiple_of(r0 * NA_COLS, NA_COLS)
            qoff = pl.multiple_of(row * GRID_W, GRID_W)
            per_tile = []
            for t in range(n_tiles):
                ts = slice(t * LANES, (t + 1) * LANES)
                qt = q_ref[0, pl.ds(qoff, GRID_W), ts].astype(F32)
                q_even = jnp.where(lower[GRID_W], qt, 0.0)
                q_odd = jnp.where(lower[GRID_W], 0.0, qt)
                qs = jnp.concatenate([part[lo:hi] for lo, hi, _ in blocks for part in (q_even, q_odd)], axis=0)
                qs = qs.astype(BF16)
                scores = [_dot_nt(qs[r_lo:r_hi], window(k_ref, wstart, ts, t_lo, t_hi))
                          for t_lo, t_hi, r_lo, r_hi in key_groups]
                per_tile.append((scores, [window(v_ref, wstart, ts, t_lo, t_hi) for t_lo, t_hi, _, _ in key_groups]))
            rows.append((shift, qoff, per_tile))
        for shift, qoff, per_tile in rows:
            tiles = []
            for t, (scores, v_windows) in enumerate(per_tile):
                seen = jnp.concatenate(
                    [jnp.concatenate([group_piece(scores, lo, hi, w + j) for j in range(NBR_WINDOW_TILES)], axis=1)
                     for lo, hi, w in blocks], axis=0) + bias_ref[shift, t]
                p = jnp.exp(seen - jnp.max(seen, axis=-1, keepdims=True))
                inv_denom = 1.0 / jnp.sum(p, axis=-1, keepdims=True)
                p = p.astype(BF16)
                outs = []
                for (t_lo, t_hi, r_lo, r_hi), v_window in zip(key_groups, v_windows):
                    strips = []
                    for lo, hi, w in blocks:
                        if not (r_lo <= 2 * lo and 2 * hi <= r_hi):
                            continue
                        strips.append(jnp.concatenate(
                            [p[2 * lo:2 * hi, (tile - w) * LANES:(tile - w + 1) * LANES]
                             if w <= tile < w + NBR_WINDOW_TILES else jnp.zeros((2 * (hi - lo), LANES), BF16)
                             for tile in range(t_lo, t_hi)], axis=1))
                    outs.append(_dot(jnp.concatenate(strips, axis=0), v_window))
                o_blocks = []
                for lo, hi, w in blocks:
                    parts = [x[2 * lo - r_lo:2 * hi - r_lo] for (_, _, r_lo, r_hi), x in zip(key_groups, outs)
                             if r_lo <= 2 * lo and 2 * hi <= r_hi]
                    o = functools.reduce(jnp.add, parts) * inv_denom[2 * lo:2 * hi]
                    o_blocks.append(jnp.where(lower[hi - lo], o[:hi - lo], o[hi - lo:]))
                tiles.append(jnp.concatenate(o_blocks, axis=0))
            ob = jnp.concatenate(tiles, axis=1)
            o_ref[0, pl.ds(qoff, GRID_W), :] = _rms(ob, g_ref[...]).astype(BF16)
        return carry

    lax.fori_loop(0, NBR_ROWS_PER_STEP // NBR_ROW_GROUP, row_group, 0)


def _window_start_col(qc):
    return np.clip(qc - NA_COLS // 2, 0, GRID_W - NA_COLS)


def _nbr_blocks():
    blocks = []
    for lo in range(0, GRID_W, SUBLANES):
        c0 = _window_start_col(np.arange(lo, lo + SUBLANES))
        first, last = int(c0.min()) // NA_COLS, (int(c0.max()) + NA_COLS - 1) // NA_COLS
        assert last - first < NBR_WINDOW_TILES
        cb = min(first, NBR_COL_BLOCKS - NBR_WINDOW_TILES)
        if blocks and blocks[-1][2] == cb:
            blocks[-1] = (blocks[-1][0], lo + SUBLANES, cb)
        else:
            blocks.append((lo, lo + SUBLANES, cb))
    return blocks


def _nbr_key_groups(blocks):
    per_group = MXU_DIM // LANES
    groups = []
    for t_lo in range(0, NBR_COL_BLOCKS, per_group):
        t_hi = t_lo + per_group
        seeing = [(2 * lo, 2 * hi) for lo, hi, w in blocks if w < t_hi and w + NBR_WINDOW_TILES > t_lo]
        groups.append((t_lo, t_hi, seeing[0][0], seeing[-1][1]))
    return groups


def _nbr_bias_table(rpb):
    qc = np.arange(GRID_W)[:, None]
    kc = np.arange(GRID_W)[None, :]
    c0 = _window_start_col(qc)
    valid = (kc >= c0) & (kc < c0 + NA_COLS)
    period = 2 * GRID_W - 1
    pad = GRID_W - NA_COLS
    v = jnp.pad(rpb.astype(F32), ((0, 0), (0, 0), (pad, pad)))
    assert v.shape[-1] == period
    skew = jnp.tile(v, (1, 1, GRID_W + 1))[..., :GRID_W * (period + 1)]
    skew = skew.reshape(v.shape[:2] + (GRID_W, period + 1))
    toeplitz = skew[:, :, ::-1, :GRID_W]
    toeplitz = jnp.where(valid[None, None], toeplitz, NEG)
    slabs = jnp.stack([toeplitz[:, t:t + NA_ROWS] for t in range(NA_ROWS)], axis=1)
    slabs = slabs.reshape(N_HEADS_B, NA_ROWS, NA_ROWS, GRID_W, NBR_COL_BLOCKS, NA_COLS)
    slabs = slabs.transpose(1, 0, 3, 4, 2, 5).reshape(NA_ROWS, N_HEADS_B // 2, 2, GRID_W, NBR_COL_BLOCKS, LANES)
    parts = []
    for lo, hi, cb in _nbr_blocks():
        seen = slabs[:, :, :, lo:hi, cb:cb + NBR_WINDOW_TILES]
        parts.append(seen.reshape(NA_ROWS, N_HEADS_B // 2, 2 * (hi - lo), NBR_WINDOW_TILES * LANES))
    return jnp.concatenate(parts, axis=2)


def _attn_nbr(qb, kb, vb, bias, gain):
    b, seq, _ = qb.shape
    grid_rows = seq // GRID_W
    tq = NBR_ROWS_PER_STEP * GRID_W
    kv_spec = pl.BlockSpec((1,) + kb.shape[1:], lambda bi, i: (bi, 0, 0, 0))
    q_spec = pl.BlockSpec((1, tq, D_B), lambda bi, i: (bi, i, 0))
    return pl.pallas_call(
        functools.partial(_attn_nbr_kernel, grid_rows=grid_rows),
        grid=(b, seq // tq),
        in_specs=[q_spec, kv_spec, kv_spec, _const_spec(bias.shape), _const_spec((1, D_B))],
        out_specs=q_spec,
        out_shape=jax.ShapeDtypeStruct((b, seq, D_B), BF16),
        compiler_params=pltpu.CompilerParams(dimension_semantics=("parallel", "arbitrary"),
                                             vmem_limit_bytes=VMEM_LIMIT),
        name="attn_nbr",
    )(qb, kb, vb, bias, gain)


def _out_ffn_kernel(x1_ref, oa_ref, ob_ref, wout_ref, mixpost_ref, pre2_ref, wgu_ref, wd_ref, post2_ref, fin_ref,
                    y_ref):
    bases = range(0, y_ref.shape[0], TOKEN_TILE)
    x2_passes = []
    for base in bases:
        mixes = [_dot(jnp.concatenate([oa, ob], axis=1), wout_ref[...])
                 for oa, ob in zip(_subtiles(oa_ref, base), _subtiles(ob_ref, base))]
        x2_passes.append([x1 + _rms(mix, mixpost_ref[...]) for x1, mix in zip(_subtiles(x1_ref, base), mixes)])
    for base, x2s in zip(bases, x2_passes):
        fs = _swiglu([_rms(x2, pre2_ref[...]).astype(BF16) for x2 in x2s], wgu_ref, wd_ref)
        x3s = [x2 + 0.5 * _rms(f, post2_ref[...]) for x2, f in zip(x2s, fs)]
        _store_subtiles(y_ref, [_rms(x3, fin_ref[...]) for x3 in x3s], base)


def _out_ffn(x1, oa, ob, wout, mixpost, pre2, wgu, wd, post2, fin):
    n_tok = x1.shape[0]
    tm = OUT_FFN_PASSES * TOKEN_TILE
    tok = lambda w: pl.BlockSpec((tm, w), lambda i: (i, 0))
    vec = _const_spec((1, D_MODEL))
    return pl.pallas_call(
        _out_ffn_kernel,
        grid=(n_tok // tm,),
        in_specs=[tok(D_MODEL), tok(D_A), tok(D_B), _const_spec(wout.shape), vec, vec, _const_spec(wgu.shape),
                  _const_spec(wd.shape), vec, vec],
        out_specs=tok(D_MODEL),
        out_shape=jax.ShapeDtypeStruct((n_tok, D_MODEL), F32),
        compiler_params=pltpu.CompilerParams(dimension_semantics=("parallel",), vmem_limit_bytes=VMEM_LIMIT),
        name="out_ffn",
    )(x1, oa, ob, wout, mixpost, pre2, wgu, wd, post2, fin)


def _rope_tables(seq):
    half = HEAD_DIM // 2
    inv = ROPE_THETA ** (-jnp.arange(0, HEAD_DIM, 2, dtype=F32) / HEAD_DIM)
    reps = 2 * LANES // HEAD_DIM
    ang = jnp.arange(seq, dtype=F32)[:, None] * jnp.tile(inv, reps)[None, :]
    sign = np.tile(np.repeat(np.array([-1.0, 1.0], np.float32), half), LANES // HEAD_DIM)
    return jnp.cos(ang), jnp.sin(ang) * sign[None, :]


def _layer(x, p):
    b, seq, _ = x.shape
    cos_t, sin_t = p["rope"]
    x1, qa, ka, va, qb, kb, vb = _ffn_in(x.reshape(b * seq, D_MODEL), seq, cos_t, sin_t, p["ffn1_pre"], p["ffn1_w_gu"],
                                         p["ffn1_w_down"], p["ffn1_post"], p["mix_pre"], p["w_in"])
    r3 = lambda a: a.reshape(b, seq, a.shape[-1])
    oa = _attn_win(r3(qa), r3(ka), r3(va), p["sink_a"], p["out_norm_a"])
    ob = _attn_nbr(r3(qb), kb, vb, p["nbr_bias"], p["out_norm_b"])
    y = _out_ffn(x1, oa.reshape(b * seq, D_A), ob.reshape(b * seq, D_B), p["w_out"], p["mix_post"], p["ffn2_pre"],
                 p["ffn2_w_gu"], p["ffn2_w_down"], p["ffn2_post"], p["final_norm"])
    return y.reshape(b, seq, D_MODEL)


def kernel(x_prompt, x_sample, ffn1_pre, ffn1_w_gu, ffn1_w_down, ffn1_post, mix_pre, w_in, sink_a, rpb_b, out_norm_a,
           out_norm_b, w_out, mix_post, ffn2_pre, ffn2_w_gu, ffn2_w_down, ffn2_post, final_norm):
    y_prompt, y_sample = x_prompt, x_sample
    group = N_HEADS_A // N_KV_A
    rope = _rope_tables(max(x_prompt.shape[1], x_sample.shape[1]))

    def regroup(a, axis):
        shape = a.shape
        a = a.reshape(shape[:axis] + (N_KV_A, group, HEAD_DIM) + shape[axis + 1:])
        return jnp.swapaxes(a, axis, axis + 1).reshape(shape)

    for l in range(ffn1_pre.shape[0]):
        row = lambda a: a[l].astype(F32).reshape(1, -1)
        w_in_l = jnp.concatenate([regroup(w_in[l][:, :D_A], 1), w_in[l][:, D_A:]], axis=1)
        w_out_l = jnp.concatenate([regroup(w_out[l][:D_A], 0), w_out[l][D_A:]], axis=0)
        p = {
            "ffn1_pre": row(ffn1_pre), "ffn1_post": row(ffn1_post), "mix_pre": row(mix_pre),
            "out_norm_a": regroup(row(out_norm_a), 1), "out_norm_b": row(out_norm_b), "mix_post": row(mix_post),
            "ffn2_pre": row(ffn2_pre), "ffn2_post": row(ffn2_post), "final_norm": row(final_norm),
            "ffn1_w_gu": ffn1_w_gu[l].astype(BF16), "ffn1_w_down": ffn1_w_down[l].astype(BF16),
            "ffn2_w_gu": ffn2_w_gu[l].astype(BF16), "ffn2_w_down": ffn2_w_down[l].astype(BF16),
            "w_in": w_in_l.astype(BF16), "w_out": w_out_l.astype(BF16),
            "sink_a": sink_a[l].astype(F32), "nbr_bias": _nbr_bias_table(rpb_b[l]), "rope": rope,
        }
        y_prompt = _layer(y_prompt, p)
        y_sample = _layer(y_sample, p)
    return (y_prompt, y_sample)
```

```python
import functools

import jax
import jax.numpy as jnp
import numpy as np
from jax import lax
from jax.experimental import pallas as pl
from jax.experimental.pallas import tpu as pltpu

D_MODEL = 1024
HEAD_DIM = 64
N_HEADS_A = 8
N_KV_A = 2
N_HEADS_B = 8
WIN = 128
BLK = 128
GRID_W = 64
NA_ROWS = 8
NA_COLS = 16
D_FF = 2816
ROPE_THETA = 10000.0
EPS = 1e-6
D_A = N_HEADS_A * HEAD_DIM
D_KV_A = N_KV_A * HEAD_DIM
D_B = N_HEADS_B * HEAD_DIM
D_IN = D_A + 2 * D_KV_A + 3 * D_B
D_ROPE = D_A + D_KV_A
NEG = -1e30
SCALE = HEAD_DIM ** -0.5
LOG2E = 1.4426950408889634

LANES = 128
TOKEN_TILE = 512
OUT_FFN_PASSES = 2
IN_FFN_PASSES = 1
TOKEN_SUBTILE_ROWS = (192, 192, 128)
WIN_Q_TILE = 1024
NBR_ROWS_PER_STEP = 8
NBR_ROW_GROUP = 8
NBR_KEYS = NA_ROWS * GRID_W
NBR_COL_BLOCKS = GRID_W // NA_COLS
NBR_WINDOW_TILES = 2
SUBLANES = 8
VMEM_BYTES = 64 * 1024 * 1024
VMEM_LIMIT = VMEM_BYTES - 8 * 1024 * 1024

BF16 = jnp.bfloat16
F32 = jnp.float32

assert NA_ROWS * NA_COLS == LANES


def _dot(a, b):
    return jnp.dot(a, b, preferred_element_type=F32)


def _dot_nt(a, b):
    return lax.dot_general(a, b, (((1,), (1,)), ((), ())), preferred_element_type=F32)


def _rms(x, g):
    return x * lax.rsqrt(jnp.mean(x * x, axis=-1, keepdims=True) + EPS) * g


def _subtile_bounds(base=0):
    edges = base + np.cumsum((0,) + TOKEN_SUBTILE_ROWS)
    return list(zip(edges[:-1], edges[1:]))


def _subtiles(ref, base=0):
    return [ref[lo:hi, :] for lo, hi in _subtile_bounds(base)]


def _store_subtiles(ref, values, base=0):
    for (lo, hi), v in zip(_subtile_bounds(base), values):
        ref[lo:hi, :] = v.astype(ref.dtype)


def _swiglu(xns, wgu_ref, wd_ref):
    hs = []
    for xn in xns:
        g = _dot(xn, wgu_ref[:, :D_FF])
        u = _dot(xn, wgu_ref[:, D_FF:])
        hs.append((g * jax.nn.sigmoid(g) * u).astype(BF16))
    return [_dot(h, wd_ref[...]) for h in hs]


def _rope_qkv(proj, cos, sin):
    lane = lax.broadcasted_iota(jnp.int32, cos.shape, 1)
    first_half = (lane % HEAD_DIM) < (HEAD_DIM // 2)
    roped = []
    for t in range(D_ROPE // LANES):
        xt = proj[:, t * LANES:(t + 1) * LANES]
        partner = jnp.where(first_half,
                            pltpu.roll(xt, LANES - HEAD_DIM // 2, 1),
                            pltpu.roll(xt, HEAD_DIM // 2, 1))
        roped.append(xt * cos + partner * sin)
    o = D_ROPE
    va = proj[:, o:o + D_KV_A]
    o += D_KV_A
    qb = proj[:, o:o + D_B] * SCALE
    o += D_B
    kb = proj[:, o:o + D_B]
    o += D_B
    vb = proj[:, o:o + D_B]
    return jnp.concatenate(roped[:D_A // LANES], axis=1) * (SCALE * LOG2E), roped[D_A // LANES], va, qb, kb, vb


def _ffn_in_kernel(x_ref, cos_ref, sin_ref, pre1_ref, wgu_ref, wd_ref, post1_ref, mixpre_ref, win_ref,
                   x1_ref, qa_ref, ka_ref, va_ref, qb_ref, kb_ref, vb_ref):
    for base in range(0, x_ref.shape[0], TOKEN_TILE):
        xs = _subtiles(x_ref, base)
        fs = _swiglu([_rms(x, pre1_ref[...]).astype(BF16) for x in xs], wgu_ref, wd_ref)
        x1s = [x + 0.5 * _rms(f, post1_ref[...]) for x, f in zip(xs, fs)]
        _store_subtiles(x1_ref, x1s, base)
        projs = [_dot(_rms(x1, mixpre_ref[...]).astype(BF16), win_ref[...]) for x1 in x1s]

        outs = [_rope_qkv(proj, cos, sin)
                for proj, cos, sin in zip(projs, _subtiles(cos_ref, base), _subtiles(sin_ref, base))]
        qa, ka, va, qb, kb, vb = zip(*outs)
        for ref, values in zip((qa_ref, ka_ref, va_ref, qb_ref), (qa, ka, va, qb)):
            _store_subtiles(ref, values, base)
        _store_col_blocked(kb_ref, kb, base // GRID_W)
        _store_col_blocked(vb_ref, vb, base // GRID_W)


def _store_col_blocked(ref, values, grid_row):
    for v in values:
        v = v.astype(ref.dtype)
        for r in range(v.shape[0] // GRID_W):
            dst = grid_row * NA_COLS
            for cb in range(NBR_COL_BLOCKS):
                src = r * GRID_W + cb * NA_COLS
                ref[0, cb, dst:dst + NA_COLS, :] = v[src:src + NA_COLS, :]
            grid_row += 1


def _const_spec(shape):
    return pl.BlockSpec(shape, lambda *_: (0,) * len(shape), pipeline_mode=pl.Buffered(1))


def _ffn_in(x2d, seq, cos_t, sin_t, pre1, wgu, wd, post1, mixpre, win):
    n_tok = x2d.shape[0]
    tm = IN_FFN_PASSES * TOKEN_TILE
    tiles_per_seq = seq // tm
    tok = lambda w: pl.BlockSpec((tm, w), lambda i: (i, 0))
    rope_spec = pl.BlockSpec((tm, LANES), lambda i: (i % tiles_per_seq, 0))
    vec = _const_spec((1, D_MODEL))
    out_widths = (D_A, D_KV_A, D_KV_A, D_B)
    blocked_rows = tm // NBR_COL_BLOCKS
    blocked_spec = pl.BlockSpec((1, NBR_COL_BLOCKS, blocked_rows, D_B),
                                lambda i: (i // tiles_per_seq, 0, i % tiles_per_seq, 0))
    blocked_shape = jax.ShapeDtypeStruct((n_tok // seq, NBR_COL_BLOCKS, seq // NBR_COL_BLOCKS, D_B), BF16)
    return pl.pallas_call(
        _ffn_in_kernel,
        grid=(n_tok // tm,),
        in_specs=[tok(D_MODEL), rope_spec, rope_spec, vec, _const_spec(wgu.shape), _const_spec(wd.shape), vec, vec,
                  _const_spec(win.shape)],
        out_specs=[tok(D_MODEL)] + [tok(w) for w in out_widths] + [blocked_spec, blocked_spec],
        out_shape=[jax.ShapeDtypeStruct((n_tok, D_MODEL), F32)]
        + [jax.ShapeDtypeStruct((n_tok, w), BF16) for w in out_widths] + [blocked_shape, blocked_shape],
        compiler_params=pltpu.CompilerParams(dimension_semantics=("parallel",), vmem_limit_bytes=VMEM_LIMIT),
        name="ffn_in",
    )(x2d, cos_t, sin_t, pre1, wgu, wd, post1, mixpre, win)


def _attn_win_kernel(sink_ref, q_ref, k_ref, v_ref, g_ref, o_ref, *, seq):
    i = pl.program_id(1)
    nk = BLK + 2 * WIN
    n_tiles = D_A // LANES
    n_blocks = WIN_Q_TILE // BLK
    lower = lax.broadcasted_iota(jnp.int32, (BLK, LANES), 1) < HEAD_DIM
    rel = lax.broadcasted_iota(jnp.int32, (BLK, nk), 1) - lax.broadcasted_iota(jnp.int32, (BLK, nk), 0)
    block_of_row = lax.broadcasted_iota(jnp.int32, (2 * n_tiles * BLK, 1), 0) // BLK
    sink = jnp.zeros((2 * n_tiles * BLK, 1), F32)
    for t in range(n_tiles):
        for u in range(2):
            sink = jnp.where(block_of_row == 2 * t + u, sink_ref[t + u * n_tiles] * LOG2E, sink)

    def window_start(j):
        q0 = i * WIN_Q_TILE + j * BLK
        start = pl.multiple_of(jnp.clip(q0 - WIN, 0, seq - nk), BLK)
        return q0, start

    scores = []
    for j in range(n_blocks):
        _, start = window_start(j)
        stacked = []
        for t in range(n_tiles):
            qt = q_ref[0, j * BLK:(j + 1) * BLK, t * LANES:(t + 1) * LANES]
            stacked += [jnp.where(lower, qt, jnp.zeros_like(qt)), jnp.where(lower, jnp.zeros_like(qt), qt)]
        scores.append(_dot_nt(jnp.concatenate(stacked, axis=0), k_ref[0, pl.ds(start, nk), :]))

    for j in range(n_blocks):
        q0, start = window_start(j)
        mask = jnp.where(jnp.abs(rel - (q0 - start)) <= WIN, 0.0, NEG)
        s = scores[j] + jnp.concatenate([mask] * (2 * n_tiles), axis=0)
        m = jnp.maximum(jnp.max(s, axis=-1, keepdims=True), sink)
        p = jnp.exp2(s - m)
        denom = jnp.sum(p, axis=-1, keepdims=True) + jnp.exp2(sink - m)
        o = _dot(p.astype(BF16), v_ref[0, pl.ds(start, nk), :]) * (1.0 / denom)
        tiles = [jnp.where(lower, o[2 * t * BLK:(2 * t + 1) * BLK], o[(2 * t + 1) * BLK:(2 * t + 2) * BLK])
                 for t in range(n_tiles)]
        oa = jnp.concatenate(tiles, axis=1)
        o_ref[0, j * BLK:(j + 1) * BLK, :] = _rms(oa, g_ref[...]).astype(BF16)


def _attn_win(qa, ka, va, sink, gain):
    b, seq, _ = qa.shape
    kv_spec = pl.BlockSpec((1, seq, D_KV_A), lambda bi, i: (bi, 0, 0))
    q_spec = pl.BlockSpec((1, WIN_Q_TILE, D_A), lambda bi, i: (bi, i, 0))
    return pl.pallas_call(
        functools.partial(_attn_win_kernel, seq=seq),
        grid=(b, seq // WIN_Q_TILE),
        in_specs=[pl.BlockSpec(memory_space=pltpu.SMEM), q_spec, kv_spec, kv_spec, _const_spec((1, D_A))],
        out_specs=q_spec,
        out_shape=jax.ShapeDtypeStruct((b, seq, D_A), BF16),
        compiler_params=pltpu.CompilerParams(dimension_semantics=("parallel", "arbitrary"),
                                             vmem_limit_bytes=VMEM_LIMIT),
        name="attn_win",
    )(sink, qa, ka, va, gain)


def _attn_nbr_kernel(q_ref, k_ref, v_ref, bias_ref, g_ref, o_ref, *, grid_rows):
    i = pl.program_id(1)
    n_tiles = D_B // LANES
    blocks = _nbr_blocks()
    lower = {n: lax.broadcasted_iota(jnp.int32, (n, LANES), 1) < HEAD_DIM
             for n in {GRID_W} | {hi - lo for lo, hi, _ in blocks}}

    def window(ref, start, ts):
        return jnp.concatenate([ref[0, cb, pl.ds(start, LANES), ts] for cb in range(NBR_COL_BLOCKS)], axis=0)

    def row_group(gi, carry):
        rows = []
        for rr in range(NBR_ROW_GROUP):
            row = gi * NBR_ROW_GROUP + rr
            r = i * NBR_ROWS_PER_STEP + row
            r0 = jnp.clip(r - NA_ROWS // 2, 0, grid_rows - NA_ROWS)
            shift = r0 - r + NA_ROWS - 1
            wstart = pl.multiple_of(r0 * NA_COLS, NA_COLS)
            qoff = pl.multiple_of(row * GRID_W, GRID_W)
            per_tile = []
            for t in range(n_tiles):
                ts = slice(t * LANES, (t + 1) * LANES)
                qt = q_ref[0, pl.ds(qoff, GRID_W), ts].astype(F32)
                q_even = jnp.where(lower[GRID_W], qt, 0.0)
                q_odd = jnp.where(lower[GRID_W], 0.0, qt)
                qs = jnp.concatenate([part[lo:hi] for lo, hi, _ in blocks for part in (q_even, q_odd)], axis=0)
                s = _dot_nt(qs.astype(BF16), window(k_ref, wstart, ts))
                per_tile.append((s, window(v_ref, wstart, ts)))
            rows.append((shift, qoff, per_tile))
        for shift, qoff, per_tile in rows:
            tiles = []
            for t, (s, v_window) in enumerate(per_tile):
                seen = jnp.concatenate([s[2 * lo:2 * hi, w * LANES:(w + NBR_WINDOW_TILES) * LANES]
                                        for lo, hi, w in blocks], axis=0) + bias_ref[shift, t]
                p = jnp.exp(seen - jnp.max(seen, axis=-1, keepdims=True))
                inv_denom = 1.0 / jnp.sum(p, axis=-1, keepdims=True)
                p = p.astype(BF16)
                zeros = lambda rows_, n: [jnp.zeros((rows_, n * LANES), BF16)] if n else []
                p_full = jnp.concatenate(
                    [jnp.concatenate(zeros(2 * (hi - lo), w) + [p[2 * lo:2 * hi]]
                                     + zeros(2 * (hi - lo), NBR_COL_BLOCKS - NBR_WINDOW_TILES - w), axis=1)
                     for lo, hi, w in blocks], axis=0)
                o = _dot(p_full, v_window) * inv_denom
                tiles.append(jnp.concatenate(
                    [jnp.where(lower[hi - lo], o[2 * lo:lo + hi], o[lo + hi:2 * hi]) for lo, hi, _ in blocks],
                    axis=0))
            ob = jnp.concatenate(tiles, axis=1)
            o_ref[0, pl.ds(qoff, GRID_W), :] = _rms(ob, g_ref[...]).astype(BF16)
        return carry

    lax.fori_loop(0, NBR_ROWS_PER_STEP // NBR_ROW_GROUP, row_group, 0)


def _window_start_col(qc):
    return np.clip(qc - NA_COLS // 2, 0, GRID_W - NA_COLS)


def _nbr_blocks():
    blocks = []
    for lo in range(0, GRID_W, SUBLANES):
        c0 = _window_start_col(np.arange(lo, lo + SUBLANES))
        first, last = int(c0.min()) // NA_COLS, (int(c0.max()) + NA_COLS - 1) // NA_COLS
        assert last - first < NBR_WINDOW_TILES
        cb = min(first, NBR_COL_BLOCKS - NBR_WINDOW_TILES)
        if blocks and blocks[-1][2] == cb:
            blocks[-1] = (blocks[-1][0], lo + SUBLANES, cb)
        else:
            blocks.append((lo, lo + SUBLANES, cb))
    return blocks


def _nbr_bias_table(rpb):
    qc = np.arange(GRID_W)[:, None]
    kc = np.arange(GRID_W)[None, :]
    c0 = _window_start_col(qc)
    valid = (kc >= c0) & (kc < c0 + NA_COLS)
    period = 2 * GRID_W - 1
    pad = GRID_W - NA_COLS
    v = jnp.pad(rpb.astype(F32), ((0, 0), (0, 0), (pad, pad)))
    assert v.shape[-1] == period
    skew = jnp.tile(v, (1, 1, GRID_W + 1))[..., :GRID_W * (period + 1)]
    skew = skew.reshape(v.shape[:2] + (GRID_W, period + 1))
    toeplitz = skew[:, :, ::-1, :GRID_W]
    toeplitz = jnp.where(valid[None, None], toeplitz, NEG)
    slabs = jnp.stack([toeplitz[:, t:t + NA_ROWS] for t in range(NA_ROWS)], axis=1)
    slabs = slabs.reshape(N_HEADS_B, NA_ROWS, NA_ROWS, GRID_W, NBR_COL_BLOCKS, NA_COLS)
    slabs = slabs.transpose(1, 0, 3, 4, 2, 5).reshape(NA_ROWS, N_HEADS_B // 2, 2, GRID_W, NBR_COL_BLOCKS, LANES)
    parts = []
    for lo, hi, cb in _nbr_blocks():
        seen = slabs[:, :, :, lo:hi, cb:cb + NBR_WINDOW_TILES]
        parts.append(seen.reshape(NA_ROWS, N_HEADS_B // 2, 2 * (hi - lo), NBR_WINDOW_TILES * LANES))
    return jnp.concatenate(parts, axis=2)


def _attn_nbr(qb, kb, vb, bias, gain):
    b, seq, _ = qb.shape
    grid_rows = seq // GRID_W
    tq = NBR_ROWS_PER_STEP * GRID_W
    kv_spec = pl.BlockSpec((1,) + kb.shape[1:], lambda bi, i: (bi, 0, 0, 0))
    q_spec = pl.BlockSpec((1, tq, D_B), lambda bi, i: (bi, i, 0))
    return pl.pallas_call(
        functools.partial(_attn_nbr_kernel, grid_rows=grid_rows),
        grid=(b, seq // tq),
        in_specs=[q_spec, kv_spec, kv_spec, _const_spec(bias.shape), _const_spec((1, D_B))],
        out_specs=q_spec,
        out_shape=jax.ShapeDtypeStruct((b, seq, D_B), BF16),
        compiler_params=pltpu.CompilerParams(dimension_semantics=("parallel", "arbitrary"),
                                             vmem_limit_bytes=VMEM_LIMIT),
        name="attn_nbr",
    )(qb, kb, vb, bias, gain)


def _out_ffn_kernel(x1_ref, oa_ref, ob_ref, wout_ref, mixpost_ref, pre2_ref, wgu_ref, wd_ref, post2_ref, fin_ref,
                    y_ref):
    bases = range(0, y_ref.shape[0], TOKEN_TILE)
    x2_passes = []
    for base in bases:
        mixes = [_dot(jnp.concatenate([oa, ob], axis=1), wout_ref[...])
                 for oa, ob in zip(_subtiles(oa_ref, base), _subtiles(ob_ref, base))]
        x2_passes.append([x1 + _rms(mix, mixpost_ref[...]) for x1, mix in zip(_subtiles(x1_ref, base), mixes)])
    for base, x2s in zip(bases, x2_passes):
        fs = _swiglu([_rms(x2, pre2_ref[...]).astype(BF16) for x2 in x2s], wgu_ref, wd_ref)
        x3s = [x2 + 0.5 * _rms(f, post2_ref[...]) for x2, f in zip(x2s, fs)]
        _store_subtiles(y_ref, [_rms(x3, fin_ref[...]) for x3 in x3s], base)


def _out_ffn(x1, oa, ob, wout, mixpost, pre2, wgu, wd, post2, fin):
    n_tok = x1.shape[0]
    tm = OUT_FFN_PASSES * TOKEN_TILE
    tok = lambda w: pl.BlockSpec((tm, w), lambda i: (i, 0))
    vec = _const_spec((1, D_MODEL))
    return pl.pallas_call(
        _out_ffn_kernel,
        grid=(n_tok // tm,),
        in_specs=[tok(D_MODEL), tok(D_A), tok(D_B), _const_spec(wout.shape), vec, vec, _const_spec(wgu.shape),
                  _const_spec(wd.shape), vec, vec],
        out_specs=tok(D_MODEL),
        out_shape=jax.ShapeDtypeStruct((n_tok, D_MODEL), F32),
        compiler_params=pltpu.CompilerParams(dimension_semantics=("parallel",), vmem_limit_bytes=VMEM_LIMIT),
        name="out_ffn",
    )(x1, oa, ob, wout, mixpost, pre2, wgu, wd, post2, fin)


def _rope_tables(seq):
    half = HEAD_DIM // 2
    inv = ROPE_THETA ** (-jnp.arange(0, HEAD_DIM, 2, dtype=F32) / HEAD_DIM)
    reps = 2 * LANES // HEAD_DIM
    ang = jnp.arange(seq, dtype=F32)[:, None] * jnp.tile(inv, reps)[None, :]
    sign = np.tile(np.repeat(np.array([-1.0, 1.0], np.float32), half), LANES // HEAD_DIM)
    return jnp.cos(ang), jnp.sin(ang) * sign[None, :]


def _layer(x, p):
    b, seq, _ = x.shape
    cos_t, sin_t = p["rope"]
    x1, qa, ka, va, qb, kb, vb = _ffn_in(x.reshape(b * seq, D_MODEL), seq, cos_t, sin_t, p["ffn1_pre"], p["ffn1_w_gu"],
                                         p["ffn1_w_down"], p["ffn1_post"], p["mix_pre"], p["w_in"])
    r3 = lambda a: a.reshape(b, seq, a.shape[-1])
    oa = _attn_win(r3(qa), r3(ka), r3(va), p["sink_a"], p["out_norm_a"])
    ob = _attn_nbr(r3(qb), kb, vb, p["nbr_bias"], p["out_norm_b"])
    y = _out_ffn(x1, oa.reshape(b * seq, D_A), ob.reshape(b * seq, D_B), p["w_out"], p["mix_post"], p["ffn2_pre"],
                 p["ffn2_w_gu"], p["ffn2_w_down"], p["ffn2_post"], p["final_norm"])
    return y.reshape(b, seq, D_MODEL)


def kernel(x_prompt, x_sample, ffn1_pre, ffn1_w_gu, ffn1_w_down, ffn1_post, mix_pre, w_in, sink_a, rpb_b, out_norm_a,
           out_norm_b, w_out, mix_post, ffn2_pre, ffn2_w_gu, ffn2_w_down, ffn2_post, final_norm):
    y_prompt, y_sample = x_prompt, x_sample
    group = N_HEADS_A // N_KV_A
    rope = _rope_tables(max(x_prompt.shape[1], x_sample.shape[1]))

    def regroup(a, axis):
        shape = a.shape
        a = a.reshape(shape[:axis] + (N_KV_A, group, HEAD_DIM) + shape[axis + 1:])
        return jnp.swapaxes(a, axis, axis + 1).reshape(shape)

    for l in range(ffn1_pre.shape[0]):
        row = lambda a: a[l].astype(F32).reshape(1, -1)
        w_in_l = jnp.concatenate([regroup(w_in[l][:, :D_A], 1), w_in[l][:, D_A:]], axis=1)
        w_out_l = jnp.concatenate([regroup(w_out[l][:D_A], 0), w_out[l][D_A:]], axis=0)
        p = {
            "ffn1_pre": row(ffn1_pre), "ffn1_post": row(ffn1_post), "mix_pre": row(mix_pre),
            "out_norm_a": regroup(row(out_norm_a), 1), "out_norm_b": row(out_norm_b), "mix_post": row(mix_post),
            "ffn2_pre": row(ffn2_pre), "ffn2_post": row(ffn2_post), "final_norm": row(final_norm),
            "ffn1_w_gu": ffn1_w_gu[l].astype(BF16), "ffn1_w_down": ffn1_w_down[l].astype(BF16),
            "ffn2_w_gu": ffn2_w_gu[l].astype(BF16), "ffn2_w_down": ffn2_w_down[l].astype(BF16),
            "w_in": w_in_l.astype(BF16), "w_out": w_out_l.astype(BF16),
            "sink_a": sink_a[l].astype(F32), "nbr_bias": _nbr_bias_table(rpb_b[l]), "rope": rope,
        }
        y_prompt = _layer(y_prompt, p)
        y_sample = _layer(y_sample, p)
    return (y_prompt, y_sample)
```

```python
import functools

import jax
import jax.numpy as jnp
import numpy as np
from jax import lax
from jax.experimental import pallas as pl
from jax.experimental.pallas import tpu as pltpu

D_MODEL = 1024
HEAD_DIM = 64
N_HEADS_A = 8
N_KV_A = 2
N_HEADS_B = 8
WIN = 128
BLK = 128
GRID_W = 64
NA_ROWS = 8
NA_COLS = 16
D_FF = 2816
ROPE_THETA = 10000.0
EPS = 1e-6
D_A = N_HEADS_A * HEAD_DIM
D_KV_A = N_KV_A * HEAD_DIM
D_B = N_HEADS_B * HEAD_DIM
D_IN = D_A + 2 * D_KV_A + 3 * D_B
D_ROPE = D_A + D_KV_A
NEG = -1e30
SCALE = HEAD_DIM ** -0.5
LOG2E = 1.4426950408889634

LANES = 128
TOKEN_TILE = 512
OUT_FFN_PASSES = 2
TOKEN_SUBTILE_ROWS = (192, 192, 128)
WIN_Q_TILE = 1024
NBR_ROWS_PER_STEP = 8
NBR_ROW_GROUP = 8
NBR_KEYS = NA_ROWS * GRID_W
NBR_COL_BLOCKS = GRID_W // NA_COLS
NBR_WINDOW_TILES = 2
SUBLANES = 8
VMEM_BYTES = 64 * 1024 * 1024
VMEM_LIMIT = VMEM_BYTES - 8 * 1024 * 1024

BF16 = jnp.bfloat16
F32 = jnp.float32

assert NA_ROWS * NA_COLS == LANES


def _dot(a, b):
    return jnp.dot(a, b, preferred_element_type=F32)


def _dot_nt(a, b):
    return lax.dot_general(a, b, (((1,), (1,)), ((), ())), preferred_element_type=F32)


def _rms(x, g):
    return x * lax.rsqrt(jnp.mean(x * x, axis=-1, keepdims=True) + EPS) * g


def _subtile_bounds(base=0):
    edges = base + np.cumsum((0,) + TOKEN_SUBTILE_ROWS)
    return list(zip(edges[:-1], edges[1:]))


def _subtiles(ref, base=0):
    return [ref[lo:hi, :] for lo, hi in _subtile_bounds(base)]


def _store_subtiles(ref, values, base=0):
    for (lo, hi), v in zip(_subtile_bounds(base), values):
        ref[lo:hi, :] = v.astype(ref.dtype)


def _swiglu(xns, wgu_ref, wd_ref):
    hs = []
    for xn in xns:
        g = _dot(xn, wgu_ref[:, :D_FF])
        u = _dot(xn, wgu_ref[:, D_FF:])
        hs.append((g * jax.nn.sigmoid(g) * u).astype(BF16))
    return [_dot(h, wd_ref[...]) for h in hs]


def _rope_qkv(proj, cos, sin):
    lane = lax.broadcasted_iota(jnp.int32, cos.shape, 1)
    first_half = (lane % HEAD_DIM) < (HEAD_DIM // 2)
    roped = []
    for t in range(D_ROPE // LANES):
        xt = proj[:, t * LANES:(t + 1) * LANES]
        partner = jnp.where(first_half,
                            pltpu.roll(xt, LANES - HEAD_DIM // 2, 1),
                            pltpu.roll(xt, HEAD_DIM // 2, 1))
        roped.append(xt * cos + partner * sin)
    o = D_ROPE
    va = proj[:, o:o + D_KV_A]
    o += D_KV_A
    qb = proj[:, o:o + D_B] * SCALE
    o += D_B
    kb = proj[:, o:o + D_B]
    o += D_B
    vb = proj[:, o:o + D_B]
    return jnp.concatenate(roped[:D_A // LANES], axis=1) * (SCALE * LOG2E), roped[D_A // LANES], va, qb, kb, vb


def _ffn_in_kernel(x_ref, cos_ref, sin_ref, pre1_ref, wgu_ref, wd_ref, post1_ref, mixpre_ref, win_ref,
                   x1_ref, qa_ref, ka_ref, va_ref, qb_ref, kb_ref, vb_ref):
    xs = _subtiles(x_ref)
    fs = _swiglu([_rms(x, pre1_ref[...]).astype(BF16) for x in xs], wgu_ref, wd_ref)
    x1s = [x + 0.5 * _rms(f, post1_ref[...]) for x, f in zip(xs, fs)]
    _store_subtiles(x1_ref, x1s)
    projs = [_dot(_rms(x1, mixpre_ref[...]).astype(BF16), win_ref[...]) for x1 in x1s]

    outs = [_rope_qkv(proj, cos, sin) for proj, cos, sin in zip(projs, _subtiles(cos_ref), _subtiles(sin_ref))]
    qa, ka, va, qb, kb, vb = zip(*outs)
    for ref, values in zip((qa_ref, ka_ref, va_ref, qb_ref), (qa, ka, va, qb)):
        _store_subtiles(ref, values)
    _store_col_blocked(kb_ref, kb)
    _store_col_blocked(vb_ref, vb)


def _store_col_blocked(ref, values):
    grid_row = 0
    for v in values:
        v = v.astype(ref.dtype)
        for r in range(v.shape[0] // GRID_W):
            dst = grid_row * NA_COLS
            for cb in range(NBR_COL_BLOCKS):
                src = r * GRID_W + cb * NA_COLS
                ref[0, cb, dst:dst + NA_COLS, :] = v[src:src + NA_COLS, :]
            grid_row += 1


def _const_spec(shape):
    return pl.BlockSpec(shape, lambda *_: (0,) * len(shape), pipeline_mode=pl.Buffered(1))


def _ffn_in(x2d, seq, cos_t, sin_t, pre1, wgu, wd, post1, mixpre, win):
    n_tok = x2d.shape[0]
    tm = TOKEN_TILE
    tiles_per_seq = seq // tm
    tok = lambda w: pl.BlockSpec((tm, w), lambda i: (i, 0))
    rope_spec = pl.BlockSpec((tm, LANES), lambda i: (i % tiles_per_seq, 0))
    vec = _const_spec((1, D_MODEL))
    out_widths = (D_A, D_KV_A, D_KV_A, D_B)
    blocked_rows = tm // NBR_COL_BLOCKS
    blocked_spec = pl.BlockSpec((1, NBR_COL_BLOCKS, blocked_rows, D_B),
                                lambda i: (i // tiles_per_seq, 0, i % tiles_per_seq, 0))
    blocked_shape = jax.ShapeDtypeStruct((n_tok // seq, NBR_COL_BLOCKS, seq // NBR_COL_BLOCKS, D_B), BF16)
    return pl.pallas_call(
        _ffn_in_kernel,
        grid=(n_tok // tm,),
        in_specs=[tok(D_MODEL), rope_spec, rope_spec, vec, _const_spec(wgu.shape), _const_spec(wd.shape), vec, vec,
                  _const_spec(win.shape)],
        out_specs=[tok(D_MODEL)] + [tok(w) for w in out_widths] + [blocked_spec, blocked_spec],
        out_shape=[jax.ShapeDtypeStruct((n_tok, D_MODEL), F32)]
        + [jax.ShapeDtypeStruct((n_tok, w), BF16) for w in out_widths] + [blocked_shape, blocked_shape],
        compiler_params=pltpu.CompilerParams(dimension_semantics=("parallel",), vmem_limit_bytes=VMEM_LIMIT),
        name="ffn_in",
    )(x2d, cos_t, sin_t, pre1, wgu, wd, post1, mixpre, win)


def _attn_win_kernel(sink_ref, q_ref, k_ref, v_ref, g_ref, o_ref, *, seq):
    i = pl.program_id(1)
    nk = BLK + 2 * WIN
    n_tiles = D_A // LANES
    n_blocks = WIN_Q_TILE // BLK
    lower = lax.broadcasted_iota(jnp.int32, (BLK, LANES), 1) < HEAD_DIM
    rel = lax.broadcasted_iota(jnp.int32, (BLK, nk), 1) - lax.broadcasted_iota(jnp.int32, (BLK, nk), 0)
    block_of_row = lax.broadcasted_iota(jnp.int32, (2 * n_tiles * BLK, 1), 0) // BLK
    sink = jnp.zeros((2 * n_tiles * BLK, 1), F32)
    for t in range(n_tiles):
        for u in range(2):
            sink = jnp.where(block_of_row == 2 * t + u, sink_ref[t + u * n_tiles] * LOG2E, sink)

    def window_start(j):
        q0 = i * WIN_Q_TILE + j * BLK
        start = pl.multiple_of(jnp.clip(q0 - WIN, 0, seq - nk), BLK)
        return q0, start

    scores = []
    for j in range(n_blocks):
        _, start = window_start(j)
        stacked = []
        for t in range(n_tiles):
            qt = q_ref[0, j * BLK:(j + 1) * BLK, t * LANES:(t + 1) * LANES]
            stacked += [jnp.where(lower, qt, jnp.zeros_like(qt)), jnp.where(lower, jnp.zeros_like(qt), qt)]
        scores.append(_dot_nt(jnp.concatenate(stacked, axis=0), k_ref[0, pl.ds(start, nk), :]))

    for j in range(n_blocks):
        q0, start = window_start(j)
        mask = jnp.where(jnp.abs(rel - (q0 - start)) <= WIN, 0.0, NEG)
        s = scores[j] + jnp.concatenate([mask] * (2 * n_tiles), axis=0)
        m = jnp.maximum(jnp.max(s, axis=-1, keepdims=True), sink)
        p = jnp.exp2(s - m)
        denom = jnp.sum(p, axis=-1, keepdims=True) + jnp.exp2(sink - m)
        o = _dot(p.astype(BF16), v_ref[0, pl.ds(start, nk), :]) * (1.0 / denom)
        tiles = [jnp.where(lower, o[2 * t * BLK:(2 * t + 1) * BLK], o[(2 * t + 1) * BLK:(2 * t + 2) * BLK])
                 for t in range(n_tiles)]
        oa = jnp.concatenate(tiles, axis=1)
        o_ref[0, j * BLK:(j + 1) * BLK, :] = _rms(oa, g_ref[...]).astype(BF16)


def _attn_win(qa, ka, va, sink, gain):
    b, seq, _ = qa.shape
    kv_spec = pl.BlockSpec((1, seq, D_KV_A), lambda bi, i: (bi, 0, 0))
    q_spec = pl.BlockSpec((1, WIN_Q_TILE, D_A), lambda bi, i: (bi, i, 0))
    return pl.pallas_call(
        functools.partial(_attn_win_kernel, seq=seq),
        grid=(b, seq // WIN_Q_TILE),
        in_specs=[pl.BlockSpec(memory_space=pltpu.SMEM), q_spec, kv_spec, kv_spec, _const_spec((1, D_A))],
        out_specs=q_spec,
        out_shape=jax.ShapeDtypeStruct((b, seq, D_A), BF16),
        compiler_params=pltpu.CompilerParams(dimension_semantics=("parallel", "arbitrary"),
                                             vmem_limit_bytes=VMEM_LIMIT),
        name="attn_win",
    )(sink, qa, ka, va, gain)


def _attn_nbr_kernel(q_ref, k_ref, v_ref, bias_ref, g_ref, o_ref, *, grid_rows):
    i = pl.program_id(1)
    n_tiles = D_B // LANES
    blocks = _nbr_blocks()
    lower = {n: lax.broadcasted_iota(jnp.int32, (n, LANES), 1) < HEAD_DIM
             for n in {GRID_W} | {hi - lo for lo, hi, _ in blocks}}

    def window(ref, start, ts):
        return jnp.concatenate([ref[0, cb, pl.ds(start, LANES), ts] for cb in range(NBR_COL_BLOCKS)], axis=0)

    def row_group(gi, carry):
        rows = []
        for rr in range(NBR_ROW_GROUP):
            row = gi * NBR_ROW_GROUP + rr
            r = i * NBR_ROWS_PER_STEP + row
            r0 = jnp.clip(r - NA_ROWS // 2, 0, grid_rows - NA_ROWS)
            shift = r0 - r + NA_ROWS - 1
            wstart = pl.multiple_of(r0 * NA_COLS, NA_COLS)
            qoff = pl.multiple_of(row * GRID_W, GRID_W)
            per_tile = []
            for t in range(n_tiles):
                ts = slice(t * LANES, (t + 1) * LANES)
                qt = q_ref[0, pl.ds(qoff, GRID_W), ts].astype(F32)
                q_even = jnp.where(lower[GRID_W], qt, 0.0)
                q_odd = jnp.where(lower[GRID_W], 0.0, qt)
                qs = jnp.concatenate([part[lo:hi] for lo, hi, _ in blocks for part in (q_even, q_odd)], axis=0)
                s = _dot_nt(qs.astype(BF16), window(k_ref, wstart, ts))
                per_tile.append((s, window(v_ref, wstart, ts)))
            rows.append((shift, qoff, per_tile))
        for shift, qoff, per_tile in rows:
            tiles = []
            for t, (s, v_window) in enumerate(per_tile):
                seen = jnp.concatenate([s[2 * lo:2 * hi, w * LANES:(w + NBR_WINDOW_TILES) * LANES]
                                        for lo, hi, w in blocks], axis=0) + bias_ref[shift, t]
                p = jnp.exp(seen - jnp.max(seen, axis=-1, keepdims=True))
                inv_denom = 1.0 / jnp.sum(p, axis=-1, keepdims=True)
                p = p.astype(BF16)
                zeros = lambda rows_, n: [jnp.zeros((rows_, n * LANES), BF16)] if n else []
                p_full = jnp.concatenate(
                    [jnp.concatenate(zeros(2 * (hi - lo), w) + [p[2 * lo:2 * hi]]
                                     + zeros(2 * (hi - lo), NBR_COL_BLOCKS - NBR_WINDOW_TILES - w), axis=1)
                     for lo, hi, w in blocks], axis=0)
                o = _dot(p_full, v_window) * inv_denom
                tiles.append(jnp.concatenate(
                    [jnp.where(lower[hi - lo], o[2 * lo:lo + hi], o[lo + hi:2 * hi]) for lo, hi, _ in blocks],
                    axis=0))
            ob = jnp.concatenate(tiles, axis=1)
            o_ref[0, pl.ds(qoff, GRID_W), :] = _rms(ob, g_ref[...]).astype(BF16)
        return carry

    lax.fori_loop(0, NBR_ROWS_PER_STEP // NBR_ROW_GROUP, row_group, 0)


def _window_start_col(qc):
    return np.clip(qc - NA_COLS // 2, 0, GRID_W - NA_COLS)


def _nbr_blocks():
    blocks = []
    for lo in range(0, GRID_W, SUBLANES):
        c0 = _window_start_col(np.arange(lo, lo + SUBLANES))
        first, last = int(c0.min()) // NA_COLS, (int(c0.max()) + NA_COLS - 1) // NA_COLS
        assert last - first < NBR_WINDOW_TILES
        cb = min(first, NBR_COL_BLOCKS - NBR_WINDOW_TILES)
        if blocks and blocks[-1][2] == cb:
            blocks[-1] = (blocks[-1][0], lo + SUBLANES, cb)
        else:
            blocks.append((lo, lo + SUBLANES, cb))
    return blocks


def _nbr_bias_table(rpb):
    qc = np.arange(GRID_W)[:, None]
    kc = np.arange(GRID_W)[None, :]
    c0 = _window_start_col(qc)
    valid = (kc >= c0) & (kc < c0 + NA_COLS)
    period = 2 * GRID_W - 1
    v = jnp.pad(rpb.astype(F32), ((0, 0), (0, 0), (0, period - rpb.shape[-1])))
    v = jnp.roll(v, -(NA_COLS - 1), axis=-1)
    skew = jnp.tile(v, (1, 1, GRID_W))[..., :GRID_W * (period - 1)]
    skew = skew.reshape(v.shape[:2] + (GRID_W, period - 1))
    toeplitz = skew[..., :GRID_W]
    toeplitz = jnp.where(valid[None, None], toeplitz, NEG)
    n_rel = 2 * NA_ROWS - 1
    flat = toeplitz.reshape(N_HEADS_B, n_rel, GRID_W * GRID_W)
    slabs = jnp.tile(flat, (1, NA_ROWS + 1, 1))[:, :NA_ROWS * (n_rel + 1)]
    slabs = slabs.reshape(N_HEADS_B, NA_ROWS, n_rel + 1, GRID_W, GRID_W)[:, :, :NA_ROWS]
    slabs = slabs.reshape(N_HEADS_B, NA_ROWS, NA_ROWS, GRID_W, NBR_COL_BLOCKS, NA_COLS)
    slabs = slabs.transpose(1, 0, 3, 4, 2, 5).reshape(NA_ROWS, N_HEADS_B // 2, 2, GRID_W, NBR_COL_BLOCKS, LANES)
    parts = []
    for lo, hi, cb in _nbr_blocks():
        seen = slabs[:, :, :, lo:hi, cb:cb + NBR_WINDOW_TILES]
        parts.append(seen.reshape(NA_ROWS, N_HEADS_B // 2, 2 * (hi - lo), NBR_WINDOW_TILES * LANES))
    return jnp.concatenate(parts, axis=2)


def _attn_nbr(qb, kb, vb, bias, gain):
    b, seq, _ = qb.shape
    grid_rows = seq // GRID_W
    tq = NBR_ROWS_PER_STEP * GRID_W
    kv_spec = pl.BlockSpec((1,) + kb.shape[1:], lambda bi, i: (bi, 0, 0, 0))
    q_spec = pl.BlockSpec((1, tq, D_B), lambda bi, i: (bi, i, 0))
    return pl.pallas_call(
        functools.partial(_attn_nbr_kernel, grid_rows=grid_rows),
        grid=(b, seq // tq),
        in_specs=[q_spec, kv_spec, kv_spec, _const_spec(bias.shape), _const_spec((1, D_B))],
        out_specs=q_spec,
        out_shape=jax.ShapeDtypeStruct((b, seq, D_B), BF16),
        compiler_params=pltpu.CompilerParams(dimension_semantics=("parallel", "arbitrary"),
                                             vmem_limit_bytes=VMEM_LIMIT),
        name="attn_nbr",
    )(qb, kb, vb, bias, gain)


def _out_ffn_kernel(x1_ref, oa_ref, ob_ref, wout_ref, mixpost_ref, pre2_ref, wgu_ref, wd_ref, post2_ref, fin_ref,
                    y_ref):
    bases = range(0, y_ref.shape[0], TOKEN_TILE)
    x2_passes = []
    for base in bases:
        mixes = [_dot(jnp.concatenate([oa, ob], axis=1), wout_ref[...])
                 for oa, ob in zip(_subtiles(oa_ref, base), _subtiles(ob_ref, base))]
        x2_passes.append([x1 + _rms(mix, mixpost_ref[...]) for x1, mix in zip(_subtiles(x1_ref, base), mixes)])
    for base, x2s in zip(bases, x2_passes):
        fs = _swiglu([_rms(x2, pre2_ref[...]).astype(BF16) for x2 in x2s], wgu_ref, wd_ref)
        x3s = [x2 + 0.5 * _rms(f, post2_ref[...]) for x2, f in zip(x2s, fs)]
        _store_subtiles(y_ref, [_rms(x3, fin_ref[...]) for x3 in x3s], base)


def _out_ffn(x1, oa, ob, wout, mixpost, pre2, wgu, wd, post2, fin):
    n_tok = x1.shape[0]
    tm = OUT_FFN_PASSES * TOKEN_TILE
    tok = lambda w: pl.BlockSpec((tm, w), lambda i: (i, 0))
    vec = _const_spec((1, D_MODEL))
    return pl.pallas_call(
        _out_ffn_kernel,
        grid=(n_tok // tm,),
        in_specs=[tok(D_MODEL), tok(D_A), tok(D_B), _const_spec(wout.shape), vec, vec, _const_spec(wgu.shape),
                  _const_spec(wd.shape), vec, vec],
        out_specs=tok(D_MODEL),
        out_shape=jax.ShapeDtypeStruct((n_tok, D_MODEL), F32),
        compiler_params=pltpu.CompilerParams(dimension_semantics=("parallel",), vmem_limit_bytes=VMEM_LIMIT),
        name="out_ffn",
    )(x1, oa, ob, wout, mixpost, pre2, wgu, wd, post2, fin)


def _rope_tables(seq):
    half = HEAD_DIM // 2
    inv = ROPE_THETA ** (-jnp.arange(0, HEAD_DIM, 2, dtype=F32) / HEAD_DIM)
    reps = 2 * LANES // HEAD_DIM
    ang = jnp.arange(seq, dtype=F32)[:, None] * jnp.tile(inv, reps)[None, :]
    sign = np.tile(np.repeat(np.array([-1.0, 1.0], np.float32), half), LANES // HEAD_DIM)
    return jnp.cos(ang), jnp.sin(ang) * sign[None, :]


def _layer(x, p):
    b, seq, _ = x.shape
    cos_t, sin_t = p["rope"]
    x1, qa, ka, va, qb, kb, vb = _ffn_in(x.reshape(b * seq, D_MODEL), seq, cos_t, sin_t, p["ffn1_pre"], p["ffn1_w_gu"],
                                         p["ffn1_w_down"], p["ffn1_post"], p["mix_pre"], p["w_in"])
    r3 = lambda a: a.reshape(b, seq, a.shape[-1])
    oa = _attn_win(r3(qa), r3(ka), r3(va), p["sink_a"], p["out_norm_a"])
    ob = _attn_nbr(r3(qb), kb, vb, p["nbr_bias"], p["out_norm_b"])
    y = _out_ffn(x1, oa.reshape(b * seq, D_A), ob.reshape(b * seq, D_B), p["w_out"], p["mix_post"], p["ffn2_pre"],
                 p["ffn2_w_gu"], p["ffn2_w_down"], p["ffn2_post"], p["final_norm"])
    return y.reshape(b, seq, D_MODEL)


def kernel(x_prompt, x_sample, ffn1_pre, ffn1_w_gu, ffn1_w_down, ffn1_post, mix_pre, w_in, sink_a, rpb_b, out_norm_a,
           out_norm_b, w_out, mix_post, ffn2_pre, ffn2_w_gu, ffn2_w_down, ffn2_post, final_norm):
    y_prompt, y_sample = x_prompt, x_sample
    group = N_HEADS_A // N_KV_A
    rope = _rope_tables(max(x_prompt.shape[1], x_sample.shape[1]))

    def regroup(a, axis):
        shape = a.shape
        a = a.reshape(shape[:axis] + (N_KV_A, group, HEAD_DIM) + shape[axis + 1:])
        return jnp.swapaxes(a, axis, axis + 1).reshape(shape)

    for l in range(ffn1_pre.shape[0]):
        row = lambda a: a[l].astype(F32).reshape(1, -1)
        w_in_l = jnp.concatenate([regroup(w_in[l][:, :D_A], 1), w_in[l][:, D_A:]], axis=1)
        w_out_l = jnp.concatenate([regroup(w_out[l][:D_A], 0), w_out[l][D_A:]], axis=0)
        p = {
            "ffn1_pre": row(ffn1_pre), "ffn1_post": row(ffn1_post), "mix_pre": row(mix_pre),
            "out_norm_a": regroup(row(out_norm_a), 1), "out_norm_b": row(out_norm_b), "mix_post": row(mix_post),
            "ffn2_pre": row(ffn2_pre), "ffn2_post": row(ffn2_post), "final_norm": row(final_norm),
            "ffn1_w_gu": ffn1_w_gu[l].astype(BF16), "ffn1_w_down": ffn1_w_down[l].astype(BF16),
            "ffn2_w_gu": ffn2_w_gu[l].astype(BF16), "ffn2_w_down": ffn2_w_down[l].astype(BF16),
            "w_in": w_in_l.astype(BF16), "w_out": w_out_l.astype(BF16),
            "sink_a": sink_a[l].astype(F32), "nbr_bias": _nbr_bias_table(rpb_b[l]), "rope": rope,
        }
        y_prompt = _layer(y_prompt, p)
        y_sample = _layer(y_sample, p)
    return (y_prompt, y_sample)
```
